```python
import math
import jax
import jax.numpy as jnp
from jax import lax
import numpy as np

D_MODEL = 1024
BATCH = 4
SEQ = 4096
DEPTH = 4
DEC_BATCH = 32
DEC_SEQ = 1
PAST_LEN = 8192
PAGE_SIZE = 128

D_INNER = 2 * D_MODEL
EPS = 1e-6
MIXER_ORDER = ('retention', 'nsa', 'gated_deltanet')

R_HEADS = 8
R_DK = D_MODEL // R_HEADS
R_DV = D_INNER // R_HEADS
R_CHUNK = 128
ROPE_BASE = 10000.0
R_IN = 2 * R_HEADS * R_DK + 2 * D_INNER

N_HEADS = 16
KV_HEADS = 4
HEAD_DIM = D_INNER // N_HEADS
CMP_LEN = 32
CMP_STRIDE = 16
CMP_HID = HEAD_DIM
SLC_BLOCK = 64
N_SELECT = 16
WINDOW = 512
Q_BLOCK = 16
NSA_KV = 2 * KV_HEADS * HEAD_DIM
NSA_IN = 2 * D_INNER + 3 * NSA_KV + 3 * N_HEADS

G_HEADS = 16
G_DK = D_INNER // G_HEADS
G_DV = D_INNER // G_HEADS
CONV_W = 4
G_CHUNK = 64
G_IN = 4 * D_INNER + 2 * G_HEADS

kernel_name = 'hybrid_retention_nsa_gdn_step'


def rms_norm(x, g):
    xf = x.astype(jnp.float32)
    y = xf * lax.rsqrt(jnp.mean(xf * xf, axis=-1, keepdims=True) + EPS)
    return (y * g.astype(jnp.float32)).astype(x.dtype)


def l2_normalize(x):
    xf = x.astype(jnp.float32)
    return xf * lax.rsqrt(jnp.sum(xf * xf, axis=-1, keepdims=True) + EPS)


def rotary(x, pos):
    half = x.shape[-1] // 2
    inv = ROPE_BASE ** (-jnp.arange(half, dtype=jnp.float32) / half)
    ang = pos.astype(jnp.float32)[:, None] * inv[None, :]
    cos, sin = jnp.cos(ang), jnp.sin(ang)
    xf = x.astype(jnp.float32)
    x1, x2 = xf[..., :half], xf[..., half:]
    return jnp.concatenate([x1 * cos - x2 * sin, x1 * sin + x2 * cos], axis=-1)


def group_norm_heads(o, g):
    of = o.astype(jnp.float32)
    xc = of - jnp.mean(of, axis=-1, keepdims=True)
    y = xc * lax.rsqrt(jnp.mean(xc * xc, axis=-1, keepdims=True) + EPS)
    return y.reshape(o.shape[0], o.shape[1], -1) * g.astype(jnp.float32)


def masked_softmax(s, mask):
    s = jnp.where(mask, s.astype(jnp.float32), -jnp.inf)
    m = jnp.max(s, axis=-1, keepdims=True)
    m = jnp.where(jnp.isfinite(m), m, 0.0)
    e = jnp.exp(s - m)
    den = jnp.sum(e, axis=-1, keepdims=True)
    return e / jnp.where(den > 0, den, 1.0)


def to_chunks(x, c, n):
    return jnp.moveaxis(x.reshape(x.shape[0], x.shape[1], n, c, *x.shape[3:]), 2, 0)


def pad_time(x, tp):
    return jnp.pad(x.astype(jnp.float32), [(0, 0), (0, 0), (0, tp - x.shape[2])] + [(0, 0)] * (x.ndim - 3))


def retention_chunked(q, k, v, log_gamma, S0):
    B, H, T, _ = q.shape
    c = min(R_CHUNK, T)
    n = -(-T // c)
    tp = n * c
    ld = jnp.where(jnp.arange(tp)[None, :] < T, log_gamma[:, None], 0.0)
    qc, kc, vc = (to_chunks(pad_time(a, tp), c, n) for a in (q, k, v))
    lc = jnp.moveaxis(ld.reshape(H, n, c), 1, 0)
    tri = jnp.tril(jnp.ones((c, c), dtype=bool))

    def step(S, inp):
        qi, ki, vi, li = inp
        g = jnp.cumsum(li, axis=-1)
        dec = jnp.exp(jnp.where(tri, g[:, :, None] - g[:, None, :], -jnp.inf))
        att = jnp.einsum('bhid,bhjd->bhij', qi, ki) * dec
        o = (jnp.einsum('bhij,bhjv->bhiv', att, vi)
             + jnp.einsum('bhid,bhdv->bhiv', qi * jnp.exp(g)[:, :, None], S))
        gl = g[:, -1]
        S = (S * jnp.exp(gl)[:, None, None]
             + jnp.einsum('bhjd,bhjv->bhdv', ki * jnp.exp(gl[:, None] - g)[:, :, None], vi))
        return S, o

    S, o = lax.scan(step, S0.astype(jnp.float32), (qc, kc, vc, lc))
    o = jnp.moveaxis(o, 0, 2).reshape(B, H, tp, -1)[:, :, :T]
    return o, S


def gated_delta_chunked(q, k, v, log_alpha, beta, S0):
    B, H, T, _ = q.shape
    dv = v.shape[-1]
    c = min(G_CHUNK, T)
    n = -(-T // c)
    tp = n * c
    qc, kc, vc, lac, bc = (to_chunks(pad_time(a, tp), c, n) for a in (q, k, v, log_alpha, beta))
    tri = jnp.tril(jnp.ones((c, c), dtype=bool))
    strict = jnp.tril(jnp.ones((c, c), dtype=bool), k=-1)

    def step(S, inp):
        qi, ki, vi, li, bi = inp
        g = jnp.cumsum(li, axis=-1)
        dec = jnp.exp(jnp.where(tri, g[..., :, None] - g[..., None, :], -jnp.inf))
        kk = jnp.einsum('bhid,bhjd->bhij', ki, ki)
        L = jnp.where(strict, bi[..., :, None] * kk * dec, 0.0)
        rhs = jnp.concatenate([vi * bi[..., None], ki * (bi * jnp.exp(g))[..., None]], axis=-1)
        sol = lax.linalg.triangular_solve(L, rhs, left_side=True, lower=True, unit_diagonal=True)
        u = sol[..., :dv] - jnp.einsum('bhik,bhkv->bhiv', sol[..., dv:], S)
        att = jnp.einsum('bhid,bhjd->bhij', qi, ki) * dec
        o = (jnp.einsum('bhid,bhdv->bhiv', qi * jnp.exp(g)[..., None], S)
             + jnp.einsum('bhij,bhjv->bhiv', att, u))
        gl = g[..., -1:]
        S = (S * jnp.exp(gl)[..., None]
             + jnp.einsum('bhjd,bhjv->bhdv', ki * jnp.exp(gl - g)[..., None], u))
        return S, o

    S, o = lax.scan(step, S0.astype(jnp.float32), (qc, kc, vc, lac, bc))
    o = jnp.moveaxis(o, 0, 2).reshape(B, H, tp, dv)[:, :, :T]
    return o, S


def selection_map(n_cmp, n_slc):
    i = np.arange(n_cmp)[:, None]
    j = np.arange(n_slc)[None, :]
    lo = np.maximum(i * CMP_STRIDE, j * SLC_BLOCK)
    hi = np.minimum(i * CMP_STRIDE + CMP_LEN, (j + 1) * SLC_BLOCK)
    return jnp.asarray(np.maximum(hi - lo, 0) / CMP_STRIDE, dtype=jnp.float32)


def retention_mixer(h, w_in, gn, w_out, S0, q0):
    B, T, _ = h.shape
    z = h @ w_in
    qk = R_HEADS * R_DK
    q = z[..., :qk].reshape(B, T, R_HEADS, R_DK).transpose(0, 2, 1, 3)
    k = z[..., qk:2 * qk].reshape(B, T, R_HEADS, R_DK).transpose(0, 2, 1, 3)
    v = z[..., 2 * qk:2 * qk + D_INNER].reshape(B, T, R_HEADS, R_DV).transpose(0, 2, 1, 3)
    gate = z[..., 2 * qk + D_INNER:]
    pos = q0 + jnp.arange(T)
    q = rotary(q, pos)
    k = rotary(k, pos) * (R_DK ** -0.5)
    log_gamma = jnp.log1p(-jnp.exp2(-5.0 - jnp.arange(R_HEADS, dtype=jnp.float32)))
    o, S = retention_chunked(q, k, v, log_gamma, S0)
    o = group_norm_heads(o.transpose(0, 2, 1, 3), gn).astype(h.dtype)
    y = (o * jax.nn.silu(gate)) @ w_out
    return y, S.astype(h.dtype)


def nsa_mixer(h, w_in, qk_norm, cmp_pe, cmp_w1, cmp_w2, w_out, q0, past_cmp, past_slc, past_win):
    B, T, _ = h.shape
    G, R, d = KV_HEADS, N_HEADS // KV_HEADS, HEAD_DIM
    z = h @ w_in
    o1 = D_INNER
    o2 = o1 + NSA_KV
    o3 = o2 + NSA_KV
    o4 = o3 + NSA_KV
    o5 = o4 + D_INNER
    q = rms_norm(z[..., :o1].reshape(B, T, G, R, d), qk_norm[0]).transpose(0, 2, 3, 1, 4)
    kv_c = z[..., o1:o2].reshape(B, T, 2, G, d)
    kv_s = z[..., o2:o3].reshape(B, T, 2, G, d)
    kv_w = z[..., o3:o4].reshape(B, T, 2, G, d)
    gate = z[..., o4:o5]
    bgate = jax.nn.sigmoid(z[..., o5:]).reshape(B, T, G, R, 3).transpose(0, 2, 3, 1, 4)
    if past_cmp is None:
        rows_c, rows_s = kv_c, kv_s
        win_pre = jnp.zeros((B, WINDOW, 2, G, d), h.dtype)
    else:
        rows_c = jnp.concatenate([past_cmp, kv_c], axis=1)
        rows_s = jnp.concatenate([past_slc, kv_s], axis=1)
        win_pre = jnp.pad(past_win, ((0, 0), (WINDOW - past_win.shape[1], 0), (0, 0), (0, 0), (0, 0)))
    tk = q0 + T

    n_cmp = (tk - CMP_LEN) // CMP_STRIDE + 1
    r = CMP_LEN // CMP_STRIDE
    ch = rows_c[:, :(n_cmp + r - 1) * CMP_STRIDE].reshape(B, n_cmp + r - 1, CMP_STRIDE, 2, G, d)
    pre = sum(jnp.einsum('bnlcgd,cldh->bncgh',
                         ch[:, j:j + n_cmp] + cmp_pe[j * CMP_STRIDE:(j + 1) * CMP_STRIDE][None, None, :, :, None, :],
                         cmp_w1[:, j * CMP_STRIDE:(j + 1) * CMP_STRIDE]) for j in range(r))
    ckv = jnp.einsum('bncgh,chd->bncgd', jax.nn.silu(pre), cmp_w2)
    ck = rms_norm(ckv[:, :, 0], qk_norm[1]).transpose(0, 2, 1, 3)
    cv = ckv[:, :, 1].transpose(0, 2, 1, 3)
    cend = jnp.arange(n_cmp) * CMP_STRIDE + CMP_LEN - 1

    n_slc = -(-tk // SLC_BLOCK)
    rs = jnp.pad(rows_s, ((0, 0), (0, n_slc * SLC_BLOCK - tk), (0, 0), (0, 0), (0, 0)))
    rs = rs.reshape(B, n_slc, SLC_BLOCK, 2, G, d)
    sk = rms_norm(rs[:, :, :, 0], qk_norm[2]).transpose(0, 3, 1, 2, 4)
    sv = rs[:, :, :, 1].transpose(0, 3, 1, 2, 4)
    smap = selection_map(n_cmp, n_slc)
    sidx = jnp.arange(n_slc)
    n_sel = min(N_SELECT, n_slc)
    gather = jax.vmap(jax.vmap(lambda blk, ix: blk[ix]))

    we = jnp.concatenate([win_pre, kv_w], axis=1)
    wk = rms_norm(we[:, :, 0], qk_norm[3]).transpose(0, 2, 1, 3)
    wv = we[:, :, 1].transpose(0, 2, 1, 3)

    qb_len = math.gcd(T, Q_BLOCK)
    scale = d ** -0.5

    def block(i):
        s0 = i * qb_len
        qb = lax.dynamic_slice_in_dim(q, s0, qb_len, axis=3)
        gb = lax.dynamic_slice_in_dim(bgate, s0, qb_len, axis=3)
        pos = q0 + s0 + jnp.arange(qb_len)
        pc = masked_softmax(jnp.einsum('bgrqd,bgnd->bgrqn', qb, ck) * scale, cend[None, :] <= pos[:, None])
        oc = jnp.einsum('bgrqn,bgnd->bgrqd', pc.astype(cv.dtype), cv)
        imp = jnp.einsum('bgrqn,ns->bgqs', pc, smap)
        cur = pos // SLC_BLOCK
        vis = sidx[None, :] <= cur[:, None]
        forced = vis & ((sidx[None, :] == 0) | (sidx[None, :] >= cur[:, None] - 1))
        imp = jnp.where(forced, jnp.inf, jnp.where(vis, imp, -jnp.inf))
        top, idx = lax.top_k(imp, n_sel)
        ks = gather(sk, idx).reshape(B, G, qb_len, n_sel * SLC_BLOCK, d)
        vs = gather(sv, idx).reshape(B, G, qb_len, n_sel * SLC_BLOCK, d)
        kpos = idx[..., None] * SLC_BLOCK + jnp.arange(SLC_BLOCK)
        ms = (top[..., None] > -jnp.inf) & (kpos <= pos[None, None, :, None, None])
        ms = ms.reshape(B, G, qb_len, n_sel * SLC_BLOCK)[:, :, None]
        ps = masked_softmax(jnp.einsum('bgrqd,bgqkd->bgrqk', qb, ks) * scale, ms)
        os_ = jnp.einsum('bgrqk,bgqkd->bgrqd', ps.astype(vs.dtype), vs)
        kw = lax.dynamic_slice_in_dim(wk, s0, WINDOW + qb_len, axis=2)
        vw = lax.dynamic_slice_in_dim(wv, s0, WINDOW + qb_len, axis=2)
        wpos = q0 + s0 - WINDOW + jnp.arange(WINDOW + qb_len)
        mw = (wpos[None, :] >= 0) & (wpos[None, :] <= pos[:, None]) & (wpos[None, :] >= pos[:, None] - WINDOW)
        pw = masked_softmax(jnp.einsum('bgrqd,bgkd->bgrqk', qb, kw) * scale, mw)
        ow = jnp.einsum('bgrqk,bgkd->bgrqd', pw.astype(vw.dtype), vw)
        return gb[..., 0:1] * oc + gb[..., 1:2] * os_ + gb[..., 2:3] * ow

    o = lax.map(block, jnp.arange(T // qb_len))
    o = o.transpose(1, 0, 4, 2, 3, 5).reshape(B, T, D_INNER)
    y = (o * jax.nn.silu(gate)) @ w_out
    if past_win is None:
        win_state = kv_w[:, max(0, T - WINDOW):]
    else:
        win_state = we[:, we.shape[1] - past_win.shape[1]:]
    return y, kv_c, kv_s, win_state


def gdn_mixer(h, w_in, conv_w, A_log, dt_bias, onorm, w_out, conv0, S0):
    B, T, _ = h.shape
    z = h @ w_in
    c3 = 3 * D_INNER
    qkv = z[..., :c3]
    gate = z[..., c3:c3 + D_INNER]
    a = z[..., c3 + D_INNER:c3 + D_INNER + G_HEADS]
    b = z[..., c3 + D_INNER + G_HEADS:]
    ext = jnp.concatenate([conv0, qkv], axis=1)
    conv = sum(ext[:, j:j + T] * conv_w[j] for j in range(CONV_W))
    qkv = jax.nn.silu(conv)
    new_conv = ext[:, T:]
    q = l2_normalize(qkv[..., :D_INNER].reshape(B, T, G_HEADS, G_DK)) * (G_DK ** -0.5)
    k = l2_normalize(qkv[..., D_INNER:2 * D_INNER].reshape(B, T, G_HEADS, G_DK))
    v = qkv[..., 2 * D_INNER:].reshape(B, T, G_HEADS, G_DV)
    beta = jax.nn.sigmoid(b.astype(jnp.float32))
    log_alpha = -jnp.exp(A_log.astype(jnp.float32)) * jax.nn.softplus(a.astype(jnp.float32) + dt_bias.astype(jnp.float32))
    o, S = gated_delta_chunked(q.transpose(0, 2, 1, 3), k.transpose(0, 2, 1, 3), v.transpose(0, 2, 1, 3),
                               log_alpha.transpose(0, 2, 1), beta.transpose(0, 2, 1), S0)
    o = rms_norm(o.transpose(0, 2, 1, 3), onorm).reshape(B, T, D_INNER).astype(h.dtype)
    y = (o * jax.nn.silu(gate)) @ w_out
    return y, new_conv, S.astype(h.dtype)


def setup_inputs(seed: int = 0) -> dict:
    key = jax.random.key(seed)
    keys = iter(jax.random.split(key, 64))
    nk = lambda: next(keys)
    f32 = jnp.float32
    n_pages = PAST_LEN // PAGE_SIZE
    n_pool = (DEC_BATCH * n_pages * 5) // 4
    win_keep = min(WINDOW, PAST_LEN)

    def w(shape, fan_in):
        return jax.random.normal(nk(), shape, f32) * fan_in ** -0.5

    def gain(shape):
        return 1.0 + 0.02 * jax.random.normal(nk(), shape, f32)

    x_prompt = jax.random.normal(nk(), (BATCH, SEQ, D_MODEL), f32)
    x_sample = jax.random.normal(nk(), (DEC_BATCH, DEC_SEQ, D_MODEL), f32)
    state_l0_ret = 0.5 * jax.random.normal(nk(), (DEC_BATCH, R_HEADS, R_DK, R_DV), f32)
    cache_l1_cmp = jax.random.normal(nk(), (n_pool, PAGE_SIZE, 2, KV_HEADS, HEAD_DIM), f32)
    cache_l1_slc = jax.random.normal(nk(), (n_pool, PAGE_SIZE, 2, KV_HEADS, HEAD_DIM), f32)
    state_l1_win = jax.random.normal(nk(), (DEC_BATCH, win_keep, 2, KV_HEADS, HEAD_DIM), f32)
    state_l2_conv = jax.random.normal(nk(), (DEC_BATCH, CONV_W - 1, 3 * D_INNER), f32)
    state_l2_gdn = 0.1 * jax.random.normal(nk(), (DEC_BATCH, G_HEADS, G_DK, G_DV), f32)
    state_l3_ret = 0.5 * jax.random.normal(nk(), (DEC_BATCH, R_HEADS, R_DK, R_DV), f32)
    page_table = jax.random.permutation(nk(), n_pool)[:DEC_BATCH * n_pages].reshape(DEC_BATCH, n_pages).astype(jnp.int32)

    dt = jnp.exp(jax.random.uniform(nk(), (G_HEADS,), f32, math.log(1e-3), math.log(1e-1)))
    return {
        'x_prompt': x_prompt, 'x_sample': x_sample,
        'state_l0_ret': state_l0_ret, 'cache_l1_cmp': cache_l1_cmp, 'cache_l1_slc': cache_l1_slc,
        'state_l1_win': state_l1_win, 'state_l2_conv': state_l2_conv, 'state_l2_gdn': state_l2_gdn,
        'state_l3_ret': state_l3_ret, 'page_table': page_table,
        'l0_norm': gain((D_MODEL,)), 'l0_w_in': w((D_MODEL, R_IN), D_MODEL),
        'l0_gn': gain((D_INNER,)), 'l0_w_out': w((D_INNER, D_MODEL), D_INNER),
        'l1_norm': gain((D_MODEL,)), 'l1_w_in': w((D_MODEL, NSA_IN), D_MODEL),
        'l1_qk_norm': gain((4, HEAD_DIM)),
        'l1_cmp_pe': 0.02 * jax.random.normal(nk(), (CMP_LEN, 2, HEAD_DIM), f32),
        'l1_cmp_w1': w((2, CMP_LEN, HEAD_DIM, CMP_HID), CMP_LEN * HEAD_DIM),
        'l1_cmp_w2': w((2, CMP_HID, HEAD_DIM), CMP_HID),
        'l1_w_out': w((D_INNER, D_MODEL), D_INNER),
        'l2_norm': gain((D_MODEL,)), 'l2_w_in': w((D_MODEL, G_IN), D_MODEL),
        'l2_conv_w': w((CONV_W, 3 * D_INNER), CONV_W),
        'l2_A_log': jnp.log(jax.random.uniform(nk(), (G_HEADS,), f32, 1.0, 16.0)),
        'l2_dt_bias': dt + jnp.log(-jnp.expm1(-dt)),
        'l2_onorm': gain((G_DV,)), 'l2_w_out': w((D_INNER, D_MODEL), D_INNER),
        'l3_norm': gain((D_MODEL,)), 'l3_w_in': w((D_MODEL, R_IN), D_MODEL),
        'l3_gn': gain((D_INNER,)), 'l3_w_out': w((D_INNER, D_MODEL), D_INNER),
    }


def reference(x_prompt, x_sample, state_l0_ret, cache_l1_cmp, cache_l1_slc, state_l1_win, state_l2_conv,
              state_l2_gdn, state_l3_ret, page_table,
              l0_norm, l0_w_in, l0_gn, l0_w_out,
              l1_norm, l1_w_in, l1_qk_norm, l1_cmp_pe, l1_cmp_w1, l1_cmp_w2, l1_w_out,
              l2_norm, l2_w_in, l2_conv_w, l2_A_log, l2_dt_bias, l2_onorm, l2_w_out,
              l3_norm, l3_w_in, l3_gn, l3_w_out):
    B = x_prompt.shape[0]
    DB = x_sample.shape[0]
    layer_params = {
        0: (l0_norm, l0_w_in, l0_gn, l0_w_out),
        1: (l1_norm, l1_w_in, l1_qk_norm, l1_cmp_pe, l1_cmp_w1, l1_cmp_w2, l1_w_out),
        2: (l2_norm, l2_w_in, l2_conv_w, l2_A_log, l2_dt_bias, l2_onorm, l2_w_out),
        3: (l3_norm, l3_w_in, l3_gn, l3_w_out),
    }
    ret_state = {0: state_l0_ret, 3: state_l3_ret}
    new = {}
    xp, xs = x_prompt, x_sample
    for i in range(DEPTH):
        kind = MIXER_ORDER[i % len(MIXER_ORDER)]
        norm, *w = layer_params[i]
        hp, hs = rms_norm(xp, norm), rms_norm(xs, norm)
        if kind == 'retention':
            yp, sp = retention_mixer(hp, *w, jnp.zeros((B, R_HEADS, R_DK, R_DV), jnp.float32), 0)
            ys, ss = retention_mixer(hs, *w, ret_state[i], PAST_LEN)
            new[i] = (sp, ss)
        elif kind == 'nsa':
            past_c = cache_l1_cmp[page_table].reshape(DB, -1, 2, KV_HEADS, HEAD_DIM)
            past_s = cache_l1_slc[page_table].reshape(DB, -1, 2, KV_HEADS, HEAD_DIM)
            yp, cp, slp, wp = nsa_mixer(hp, *w, 0, None, None, None)
            ys, cs, sls, ws = nsa_mixer(hs, *w, PAST_LEN, past_c, past_s, state_l1_win)
            new[i] = (cp, cs, slp, sls, wp, ws)
        else:
            yp, cvp, gp = gdn_mixer(hp, *w, jnp.zeros((B, CONV_W - 1, 3 * D_INNER), x_prompt.dtype),
                                    jnp.zeros((B, G_HEADS, G_DK, G_DV), jnp.float32))
            ys, cvs, gs = gdn_mixer(hs, *w, state_l2_conv, state_l2_gdn)
            new[i] = (cvp, cvs, gp, gs)
        xp, xs = xp + yp, xs + ys
    return (xp, xs,
            new[0][0], new[0][1],
            new[1][0], new[1][1], new[1][2], new[1][3], new[1][4], new[1][5],
            new[2][0], new[2][1], new[2][2], new[2][3],
            new[3][0], new[3][1])
```

```python
import functools
import math

import jax
import jax.numpy as jnp
import numpy as np
from jax import lax
from jax.experimental import pallas as pl
from jax.experimental.pallas import tpu as pltpu

F32 = jnp.float32
BF = jnp.bfloat16
EPS = 1e-6
LANES = 128
VMEM_LIMIT = 56 * 2**20

D_MODEL = 1024
D_INNER = 2 * D_MODEL
PAST_LEN = 8192
PAGE_SIZE = 128
R_HEADS, R_DK, R_DV = 8, 128, 256
ROPE_BASE = 10000.0
N_HEADS, KV_HEADS, HEAD_DIM = 16, 4, 128
CMP_LEN, CMP_STRIDE, SLC_BLOCK, N_SELECT, WINDOW = 32, 16, 64, 16, 512
NSA_KV = 2 * KV_HEADS * HEAD_DIM
G_HEADS, G_DK, G_DV, CONV_W, G_CHUNK = 16, 128, 128, 4, 64


def _params(*sem):
    return pltpu.CompilerParams(dimension_semantics=sem, vmem_limit_bytes=VMEM_LIMIT)


def _dot(a, b):
    return jnp.dot(a, b, preferred_element_type=F32)


def _dot_nt(a, b):
    return lax.dot_general(a, b, (((1,), (1,)), ((), ())), preferred_element_type=F32)


def _dot_tn(a, b):
    return lax.dot_general(a, b, (((0,), (0,)), ((), ())), preferred_element_type=F32)


def _silu(x):
    return x * jax.nn.sigmoid(x)


def _proj_kernel(x_ref, g_ref, w_ref, o_ref, xn_ref):
    @pl.when(pl.program_id(1) == 0)
    def _():
        x = x_ref[...]
        ms = jnp.mean(x * x, axis=-1, keepdims=True)
        xn_ref[...] = (x * lax.rsqrt(ms + EPS) * g_ref[...]).astype(BF)

    o_ref[...] = _dot(xn_ref[...], w_ref[...])


def _norm_proj(x, gain, w_bf, tn):
    M, D = x.shape
    N = w_bf.shape[1]
    tm = min(M, 1024)
    return pl.pallas_call(
        _proj_kernel,
        grid=(M // tm, N // tn),
        in_specs=[pl.BlockSpec((tm, D), lambda i, j: (i, 0)),
                  pl.BlockSpec((1, D), lambda i, j: (0, 0)),
                  pl.BlockSpec((D, tn), lambda i, j: (0, j))],
        out_specs=pl.BlockSpec((tm, tn), lambda i, j: (i, j)),
        out_shape=jax.ShapeDtypeStruct((M, N), F32),
        scratch_shapes=[pltpu.VMEM((tm, D), BF)],
        compiler_params=_params("parallel", "arbitrary"),
    )(x, gain.reshape(1, D), w_bf)


def _out_kernel(a_ref, w_ref, x_ref, o_ref):
    o_ref[...] = x_ref[...] + _dot(a_ref[...], w_ref[...])


def _out_proj(a_bf, w_bf, x):
    M, K = a_bf.shape
    N = w_bf.shape[1]
    tm = min(M, 1024)
    tn = 512
    return pl.pallas_call(
        _out_kernel,
        grid=(M // tm, N // tn),
        in_specs=[pl.BlockSpec((tm, K), lambda i, j: (i, 0)),
                  pl.BlockSpec((K, tn), lambda i, j: (0, j)),
                  pl.BlockSpec((tm, tn), lambda i, j: (i, j))],
        out_specs=pl.BlockSpec((tm, tn), lambda i, j: (i, j)),
        out_shape=jax.ShapeDtypeStruct((M, N), F32),
        compiler_params=_params("parallel", "arbitrary"),
    )(a_bf, w_bf, x)


def _rope_tables(pos):
    half = R_DK // 2
    inv = ROPE_BASE ** (-jnp.arange(half, dtype=F32) / half)
    ang = pos.astype(F32)[:, None] * inv[None, :]
    cos, sin = jnp.cos(ang), jnp.sin(ang)
    return jnp.concatenate([cos, cos], axis=-1), jnp.concatenate([-sin, sin], axis=-1)


def _log_gamma():
    return jnp.log1p(-jnp.exp2(-5.0 - jnp.arange(R_HEADS, dtype=F32)))


def _group_norm_gate(o, gn, gate):
    mu = jnp.mean(o, axis=-1, keepdims=True)
    xc = o - mu
    var = jnp.mean(xc * xc, axis=-1, keepdims=True)
    return (xc * lax.rsqrt(var + EPS) * gn * _silu(gate)).astype(BF)


def _ret_kernel(lg_ref, q_ref, k_ref, v_ref, gt_ref, cos_ref, sin_ref, gn_ref, o_ref, s_ref, *, c, nchunk):
    h = pl.program_id(1)
    lg = lg_ref[h]

    @pl.when(pl.program_id(2) == 0)
    def _():
        s_ref[...] = jnp.zeros_like(s_ref)

    ii = lax.broadcasted_iota(jnp.int32, (c, c), 0)
    jj = lax.broadcasted_iota(jnp.int32, (c, c), 1)
    dec = jnp.where(ii >= jj, jnp.exp((ii - jj).astype(F32) * lg), 0.0)
    ridx = lax.broadcasted_iota(jnp.int32, (c, 1), 0).astype(F32)
    eg = jnp.exp((ridx + 1.0) * lg)
    ek = jnp.exp((c - 1.0 - ridx) * lg)
    egl = jnp.exp(jnp.full((1, 1), c, F32) * lg)
    gn = gn_ref[...]
    for ci in range(nchunk):
        sl = pl.ds(ci * c, c)
        cos, sin = cos_ref[sl, :], sin_ref[sl, :]
        q, k = q_ref[sl, :], k_ref[sl, :]
        qr = q * cos + pltpu.roll(q, R_DK // 2, 1) * sin
        kr = (k * cos + pltpu.roll(k, R_DK // 2, 1) * sin) * (R_DK ** -0.5)
        v = v_ref[sl, :].astype(BF)
        s = s_ref[0, 0]
        att = _dot_nt(qr.astype(BF), kr.astype(BF)) * dec
        o = _dot(att.astype(BF), v) + _dot((qr * eg).astype(BF), s.astype(BF))
        s_ref[0, 0] = s * egl + _dot_tn((kr * ek).astype(BF), v)
        o_ref[sl, :] = _group_norm_gate(o, gn, gt_ref[sl, :])


def _retention_prompt(z, B, T, gn):
    c = 256
    tb = 512
    nt = T // tb
    cos, sin = _rope_tables(jnp.arange(T))
    kq, kv = R_HEADS, (2 * R_HEADS * R_DK) // R_DV
    return pl.pallas_call(
        functools.partial(_ret_kernel, c=c, nchunk=tb // c),
        grid=(B, R_HEADS, nt),
        in_specs=[pl.BlockSpec(memory_space=pltpu.SMEM),
                  pl.BlockSpec((tb, R_DK), lambda b, h, i: (b * nt + i, h)),
                  pl.BlockSpec((tb, R_DK), lambda b, h, i: (b * nt + i, kq + h)),
                  pl.BlockSpec((tb, R_DV), lambda b, h, i: (b * nt + i, kv + h)),
                  pl.BlockSpec((tb, R_DV), lambda b, h, i: (b * nt + i, kv + R_HEADS + h)),
                  pl.BlockSpec((tb, R_DK), lambda b, h, i: (i, 0)),
                  pl.BlockSpec((tb, R_DK), lambda b, h, i: (i, 0)),
                  pl.BlockSpec((1, R_DV), lambda b, h, i: (0, h))],
        out_specs=[pl.BlockSpec((tb, R_DV), lambda b, h, i: (b * nt + i, h)),
                   pl.BlockSpec((1, 1, R_DK, R_DV), lambda b, h, i: (b, h, 0, 0))],
        out_shape=[jax.ShapeDtypeStruct((B * T, D_INNER), BF),
                   jax.ShapeDtypeStruct((B, R_HEADS, R_DK, R_DV), F32)],
        compiler_params=_params("parallel", "parallel", "arbitrary"),
    )(_log_gamma(), z, z, z, z, cos, sin, gn.reshape(1, D_INNER))


SUB = 8


def _ret_step_kernel(lg_ref, z_ref, s0_ref, cos_ref, sin_ref, gn_ref, o_ref, s_ref):
    cos, sin = cos_ref[...], sin_ref[...]
    qk = R_HEADS * R_DK
    for h in range(R_HEADS):
        gam = jnp.exp(jnp.full((1, 1), 1.0, F32) * lg_ref[h])
        q = z_ref[0, :, h * R_DK:(h + 1) * R_DK]
        k = z_ref[0, :, qk + h * R_DK:qk + (h + 1) * R_DK]
        v = z_ref[0, :, 2 * qk + h * R_DV:2 * qk + (h + 1) * R_DV]
        gate = z_ref[0, :, 2 * qk + D_INNER + h * R_DV:2 * qk + D_INNER + (h + 1) * R_DV]
        qr = q * cos + pltpu.roll(q, R_DK // 2, 1) * sin
        kr = (k * cos + pltpu.roll(k, R_DK // 2, 1) * sin) * (R_DK ** -0.5)
        s = s0_ref[0, h]
        att = jnp.sum(qr * kr, axis=-1, keepdims=True)
        o = att * v + _dot((qr * gam).astype(BF), s.astype(BF))
        s_ref[0, h] = s * gam + _dot_tn(kr.astype(BF), v.astype(BF))
        o_ref[0, :, h * R_DV:(h + 1) * R_DV] = _group_norm_gate(o, gn_ref[:, h * R_DV:(h + 1) * R_DV], gate)


def _retention_step(z, s0, gn):
    DB, N = z.shape
    zp = jnp.pad(z[:, None, :], ((0, 0), (0, SUB - 1), (0, 0)))
    cos, sin = _rope_tables(jnp.full((1,), PAST_LEN))
    o, s = pl.pallas_call(
        _ret_step_kernel,
        grid=(DB,),
        in_specs=[pl.BlockSpec(memory_space=pltpu.SMEM),
                  pl.BlockSpec((1, SUB, N), lambda b: (b, 0, 0)),
                  pl.BlockSpec((1, R_HEADS, R_DK, R_DV), lambda b: (b, 0, 0, 0)),
                  pl.BlockSpec((1, R_DK), lambda b: (0, 0)),
                  pl.BlockSpec((1, R_DK), lambda b: (0, 0)),
                  pl.BlockSpec((1, D_INNER), lambda b: (0, 0))],
        out_specs=[pl.BlockSpec((1, SUB, D_INNER), lambda b: (b, 0, 0)),
                   pl.BlockSpec((1, R_HEADS, R_DK, R_DV), lambda b: (b, 0, 0, 0))],
        out_shape=[jax.ShapeDtypeStruct((DB, SUB, D_INNER), BF),
                   jax.ShapeDtypeStruct(s0.shape, F32)],
        compiler_params=_params("parallel"),
    )(_log_gamma(), zp, s0, cos, sin, gn.reshape(1, D_INNER))
    return o[:, 0, :], s


def _retention_layer(xp, xs, norm, w_in, gn, w_out, s0):
    B, T, D = xp.shape
    DB = xs.shape[0]
    w_in_bf, w_out_bf = w_in.astype(BF), w_out.astype(BF)
    zp = _norm_proj(xp.reshape(B * T, D), norm, w_in_bf, 512)
    zs = _norm_proj(xs.reshape(DB, D), norm, w_in_bf, 512)
    op, sp = _retention_prompt(zp, B, T, gn)
    os_, ss = _retention_step(zs, s0, gn)
    xp = _out_proj(op, w_out_bf, xp.reshape(B * T, D)).reshape(B, T, D)
    xs = _out_proj(os_, w_out_bf, xs.reshape(DB, D)).reshape(DB, 1, D)
    return xp, xs, sp, ss


def _conv_kernel(x_ref, halo_ref, w_ref, o_ref, ext_ref, *, tb):
    sec = pl.program_id(2)
    first = pl.program_id(1) == 0
    ext_ref[pl.ds(SUB, tb), :] = x_ref[...]
    ext_ref[pl.ds(0, SUB), :] = jnp.where(first, 0.0, halo_ref[...])
    conv = ext_ref[pl.ds(SUB - 3, tb), :] * w_ref[0:1, :]
    for j in range(1, CONV_W):
        conv = conv + ext_ref[pl.ds(SUB - 3 + j, tb), :] * w_ref[j:j + 1, :]
    y = _silu(conv)
    qscale = jnp.where(sec == 0, G_DK ** -0.5, 1.0)
    for h in range(D_INNER // G_DK):
        yh = y[:, h * G_DK:(h + 1) * G_DK]
        r = lax.rsqrt(jnp.sum(yh * yh, axis=-1, keepdims=True) + EPS) * qscale
        o_ref[:, h * G_DK:(h + 1) * G_DK] = yh * jnp.where(sec == 2, 1.0, r)


def _gdn_conv(z, B, T, conv_w):
    tb = 256
    nt = T // tb
    hb = tb // SUB
    return pl.pallas_call(
        functools.partial(_conv_kernel, tb=tb),
        grid=(B, nt, 3),
        in_specs=[pl.BlockSpec((tb, D_INNER), lambda b, i, s: (b * nt + i, s)),
                  pl.BlockSpec((SUB, D_INNER), lambda b, i, s: (jnp.maximum((b * nt + i) * hb - 1, 0), s)),
                  pl.BlockSpec((CONV_W, D_INNER), lambda b, i, s: (0, s))],
        out_specs=pl.BlockSpec((tb, D_INNER), lambda b, i, s: (b * nt + i, s)),
        out_shape=jax.ShapeDtypeStruct((B * T, 3 * D_INNER), F32),
        scratch_shapes=[pltpu.VMEM((tb + SUB, D_INNER), F32)],
        compiler_params=_params("parallel", "parallel", "parallel"),
    )(z, z, conv_w)


def _split_bf(x):
    hi = x.astype(BF)
    return hi, (x - hi.astype(F32)).astype(BF)


def _dot3(a, b):
    ah, al = _split_bf(a)
    bh, bl = _split_bf(b)
    return _dot(ah, bh) + (_dot(ah, bl) + _dot(al, bh))


def _softplus(x):
    return jnp.maximum(x, 0.0) + jnp.log(1.0 + jnp.exp(-jnp.abs(x)))


def _unit_lower_inverse(lmat, eye):
    c = lmat.shape[0]
    x = -lmat
    t = eye + x
    for _ in range(int(math.log2(c)) - 1):
        x = _dot3(x, x)
        t = t + _dot3(t, x)
    return t


def _rms_gate(o, gain, gate):
    ms = jnp.mean(o * o, axis=-1, keepdims=True)
    return (o * lax.rsqrt(ms + EPS) * gain * _silu(gate)).astype(BF)


def _gdn_kernel(al_ref, dt_ref, q_ref, k_ref, v_ref, gt_ref, abc_ref, abr_ref, on_ref, o_ref, s_ref, *, c, nchunk):
    h = pl.program_id(1)

    @pl.when(pl.program_id(2) == 0)
    def _():
        s_ref[...] = jnp.zeros_like(s_ref)

    neg_a = -jnp.exp(jnp.full((1, 1), 1.0, F32) * al_ref[h])
    dt = dt_ref[h]
    ii = lax.broadcasted_iota(jnp.int32, (c, c), 0)
    jj = lax.broadcasted_iota(jnp.int32, (c, c), 1)
    tri = ii >= jj
    eye = (ii == jj).astype(F32)
    lane = lax.broadcasted_iota(jnp.int32, (c, LANES), 1)
    pre = []
    for ci in range(nchunk):
        sl = pl.ds(ci * c, c)
        ab = abc_ref[sl, :]
        a_col = jnp.sum(jnp.where(lane == h, ab, 0.0), axis=-1, keepdims=True)
        b_col = jnp.sum(jnp.where(lane == G_HEADS + h, ab, 0.0), axis=-1, keepdims=True)
        a_row = abr_ref[0, 0, :, sl]
        la_col = neg_a * _softplus(a_col + dt)
        la_row = neg_a * _softplus(a_row + dt)
        beta = jax.nn.sigmoid(b_col)
        g_col = jnp.sum(jnp.where(tri, la_row, 0.0), axis=1, keepdims=True)
        g_row = jnp.sum(jnp.where(ii <= jj, la_col, 0.0), axis=0, keepdims=True)
        gl = jnp.sum(la_row, axis=1, keepdims=True)
        dec = jnp.where(tri, jnp.exp(g_col - g_row), 0.0)
        q, k, v = q_ref[sl, :], k_ref[sl, :], v_ref[sl, :]
        kb = k.astype(BF)
        kk = _dot_nt(kb, kb)
        lmat = jnp.where(ii > jj, beta * kk * dec, 0.0)
        tinv = _unit_lower_inverse(lmat, eye)
        th, tl = _split_bf(tinv)
        rv = (v * beta).astype(BF)
        rk = (k * (beta * jnp.exp(g_col))).astype(BF)
        u0 = _dot(th, rv) + _dot(tl, rv)
        w = _dot(th, rk) + _dot(tl, rk)
        att = _dot_nt(q.astype(BF), kb) * dec
        qg = (q * jnp.exp(g_col)).astype(BF)
        kd = (k * jnp.exp(gl - g_col)).astype(BF)
        pre.append((u0, w.astype(BF), att.astype(BF), qg, kd, jnp.exp(gl)))
    s = s_ref[0, 0]
    gain = on_ref[...]
    for ci in range(nchunk):
        sl = pl.ds(ci * c, c)
        u0, w, att, qg, kd, egl = pre[ci]
        sb = s.astype(BF)
        u = u0 - _dot(w, sb)
        ub = u.astype(BF)
        o = _dot(qg, sb) + _dot(att, ub)
        s = s * egl + _dot_tn(kd, ub)
        o_ref[sl, :] = _rms_gate(o, gain, gt_ref[sl, :])
    s_ref[0, 0] = s


def _gdn_prompt(qkv, z, zab, B, T, a_log, dt_bias, onorm):
    c = G_CHUNK
    tb = 256
    nt = T // tb
    H = G_HEADS
    zab_rows = zab[:, :H].reshape(B, T, H).transpose(0, 2, 1)[:, :, None, :]
    return pl.pallas_call(
        functools.partial(_gdn_kernel, c=c, nchunk=tb // c),
        grid=(B, H, nt),
        in_specs=[pl.BlockSpec(memory_space=pltpu.SMEM),
                  pl.BlockSpec(memory_space=pltpu.SMEM),
                  pl.BlockSpec((tb, G_DK), lambda b, h, i: (b * nt + i, h)),
                  pl.BlockSpec((tb, G_DK), lambda b, h, i: (b * nt + i, H + h)),
                  pl.BlockSpec((tb, G_DV), lambda b, h, i: (b * nt + i, 2 * H + h)),
                  pl.BlockSpec((tb, G_DV), lambda b, h, i: (b * nt + i, 3 * H + h)),
                  pl.BlockSpec((tb, LANES), lambda b, h, i: (b * nt + i, 0)),
                  pl.BlockSpec((1, 1, 1, tb), lambda b, h, i: (b, h, 0, i)),
                  pl.BlockSpec((1, G_DV), lambda b, h, i: (0, 0))],
        out_specs=[pl.BlockSpec((tb, G_DV), lambda b, h, i: (b * nt + i, h)),
                   pl.BlockSpec((1, 1, G_DK, G_DV), lambda b, h, i: (b, h, 0, 0))],
        out_shape=[jax.ShapeDtypeStruct((B * T, D_INNER), BF),
                   jax.ShapeDtypeStruct((B, H, G_DK, G_DV), F32)],
        compiler_params=_params("parallel", "parallel", "arbitrary"),
    )(a_log, dt_bias, qkv, qkv, qkv, z, zab, zab_rows, onorm.reshape(1, G_DV))


def _gdn_step_kernel(al_ref, dt_ref, z_ref, ab_ref, c0_ref, w_ref, s0_ref, on_ref, o_ref, s_ref):
    row0 = lax.broadcasted_iota(jnp.int32, (SUB, 1), 0) == 0
    lane = lax.broadcasted_iota(jnp.int32, (1, LANES), 1)
    ab = ab_ref[0]
    c3 = 3 * D_INNER

    def conv_head(col):
        sl = slice(col * G_DK, (col + 1) * G_DK)
        acc = z_ref[0, :, sl] * w_ref[CONV_W - 1:CONV_W, sl]
        for j in range(CONV_W - 1):
            acc = acc + c0_ref[0, j:j + 1, sl] * w_ref[j:j + 1, sl]
        return jnp.where(row0, _silu(acc), 0.0)

    for h in range(G_HEADS):
        neg_a = -jnp.exp(jnp.full((1, 1), 1.0, F32) * al_ref[h])
        a = jnp.sum(jnp.where(lane == h, ab, 0.0), axis=-1, keepdims=True)
        b = jnp.sum(jnp.where(lane == G_HEADS + h, ab, 0.0), axis=-1, keepdims=True)
        eg = jnp.exp(neg_a * _softplus(a + dt_ref[h]))
        beta = jax.nn.sigmoid(b)
        q, k, v = conv_head(h), conv_head(G_HEADS + h), conv_head(2 * G_HEADS + h)
        q = q * lax.rsqrt(jnp.sum(q * q, axis=-1, keepdims=True) + EPS) * (G_DK ** -0.5)
        k = k * lax.rsqrt(jnp.sum(k * k, axis=-1, keepdims=True) + EPS)
        s = s0_ref[0, h]
        sb = s.astype(BF)
        u = v * beta - _dot((k * (beta * eg)).astype(BF), sb)
        att = jnp.sum(q * k, axis=-1, keepdims=True)
        o = _dot((q * eg).astype(BF), sb) + att * u
        s_ref[0, h] = s * eg + _dot_tn(k.astype(BF), u.astype(BF))
        gate = z_ref[0, :, c3 + h * G_DV:c3 + (h + 1) * G_DV]
        o_ref[0, :, h * G_DV:(h + 1) * G_DV] = _rms_gate(o, on_ref[...], gate)


def _gdn_step(z, zab, conv0, conv_w, s0, a_log, dt_bias, onorm):
    DB, N = z.shape
    zp = jnp.pad(z[:, None, :], ((0, 0), (0, SUB - 1), (0, 0)))
    o, s = pl.pallas_call(
        _gdn_step_kernel,
        grid=(DB,),
        in_specs=[pl.BlockSpec(memory_space=pltpu.SMEM),
                  pl.BlockSpec(memory_space=pltpu.SMEM),
                  pl.BlockSpec((1, SUB, N), lambda b: (b, 0, 0)),
                  pl.BlockSpec((1, 1, LANES), lambda b: (b, 0, 0)),
                  pl.BlockSpec((1, CONV_W - 1, 3 * D_INNER), lambda b: (b, 0, 0)),
                  pl.BlockSpec((CONV_W, 3 * D_INNER), lambda b: (0, 0)),
                  pl.BlockSpec((1, G_HEADS, G_DK, G_DV), lambda b: (b, 0, 0, 0)),
                  pl.BlockSpec((1, G_DV), lambda b: (0, 0))],
        out_specs=[pl.BlockSpec((1, SUB, D_INNER), lambda b: (b, 0, 0)),
                   pl.BlockSpec((1, G_HEADS, G_DK, G_DV), lambda b: (b, 0, 0, 0))],
        out_shape=[jax.ShapeDtypeStruct((DB, SUB, D_INNER), BF),
                   jax.ShapeDtypeStruct(s0.shape, F32)],
        compiler_params=_params("parallel"),
    )(a_log, dt_bias, zp, zab[:, None, :], conv0, conv_w, s0, onorm.reshape(1, G_DV))
    return o[:, 0, :], s


def _pad_cols(w, n):
    return jnp.pad(w, ((0, 0), (0, n - w.shape[1])))


def _gdn_layer(xp, xs, norm, w_in, conv_w, a_log, dt_bias, onorm, w_out, conv0, s0):
    B, T, D = xp.shape
    DB = xs.shape[0]
    c3, c4 = 3 * D_INNER, 4 * D_INNER
    w_main = w_in[:, :c4].astype(BF)
    w_tail = _pad_cols(w_in[:, c4:], LANES).astype(BF)
    w_out_bf = w_out.astype(BF)
    xp2, xs2 = xp.reshape(B * T, D), xs.reshape(DB, D)
    zp, zs = _norm_proj(xp2, norm, w_main, 512), _norm_proj(xs2, norm, w_main, 512)
    abp, abs_ = _norm_proj(xp2, norm, w_tail, LANES), _norm_proj(xs2, norm, w_tail, LANES)
    qkv = _gdn_conv(zp, B, T, conv_w)
    op, sp = _gdn_prompt(qkv, zp, abp, B, T, a_log, dt_bias, onorm)
    os_, ss = _gdn_step(zs, abs_, conv0, conv_w, s0, a_log, dt_bias, onorm)
    conv_p = zp.reshape(B, T, c4)[:, T - (CONV_W - 1):, :c3]
    conv_s = jnp.concatenate([conv0[:, 1:], zs[:, None, :c3]], axis=1)
    xp = _out_proj(op, w_out_bf, xp2).reshape(B, T, D)
    xs = _out_proj(os_, w_out_bf, xs2).reshape(DB, 1, D)
    return xp, xs, conv_p, conv_s, sp, ss


NEG = -1e30
BIG = 3e38
GQA = N_HEADS // KV_HEADS
CHUNKS_PER_PAGE = PAGE_SIZE // CMP_STRIDE
PAGES_PER_STEP = 8
KV_ROWS = 2 * KV_HEADS


def _rms(x, gain):
    ms = jnp.mean(x * x, axis=-1, keepdims=True)
    return x * lax.rsqrt(ms + EPS) * gain


def _masked_softmax(s, mask):
    m = jnp.max(jnp.where(mask, s, NEG), axis=-1, keepdims=True)
    e = jnp.where(mask, jnp.exp(s - m), 0.0)
    den = jnp.sum(e, axis=-1, keepdims=True)
    return e / jnp.where(den > 0, den, 1.0)


def _cmp1_kernel(pt_ref, *refs):
    del pt_ref
    pages, w_ref, o_ref = refs[:PAGES_PER_STEP], refs[PAGES_PER_STEP], refs[PAGES_PER_STEP + 1]
    npc = PAGES_PER_STEP * CHUNKS_PER_PAGE
    for c in range(2):
        acc = jnp.zeros((KV_HEADS * npc, 2 * HEAD_DIM), F32)
        for l in range(CMP_STRIDE):
            parts = []
            for g in range(KV_HEADS):
                first = l * KV_ROWS + c * KV_HEADS + g
                for p in range(PAGES_PER_STEP):
                    parts.append(pages[p][pl.ds(first, CHUNKS_PER_PAGE, stride=CMP_STRIDE * KV_ROWS), :])
            lhs = jnp.concatenate(parts, axis=0).astype(BF)
            acc = acc + _dot(lhs, w_ref[c, l])
        for g in range(KV_HEADS):
            o_ref[c, g] = acc[g * npc:(g + 1) * npc]


def _cmp_stage1(rows, page_ids, w1cat):
    n = page_ids.shape[0]
    steps = n // PAGES_PER_STEP
    npc = PAGES_PER_STEP * CHUNKS_PER_PAGE
    page_spec = lambda p: pl.BlockSpec((None, PAGE_SIZE * KV_ROWS, HEAD_DIM),
                                       lambda s, pt: (pt[s * PAGES_PER_STEP + p], 0, 0))
    return pl.pallas_call(
        _cmp1_kernel,
        grid_spec=pltpu.PrefetchScalarGridSpec(
            num_scalar_prefetch=1,
            grid=(steps,),
            in_specs=[page_spec(p) for p in range(PAGES_PER_STEP)]
            + [pl.BlockSpec((2, CMP_STRIDE, HEAD_DIM, 2 * HEAD_DIM), lambda s, pt: (0, 0, 0, 0))],
            out_specs=pl.BlockSpec((2, KV_HEADS, npc, 2 * HEAD_DIM), lambda s, pt: (0, 0, s, 0)),
        ),
        out_shape=jax.ShapeDtypeStruct((2, KV_HEADS, n * CHUNKS_PER_PAGE, 2 * HEAD_DIM), F32),
        compiler_params=_params("parallel"),
    )(page_ids, *([rows] * PAGES_PER_STEP), w1cat)


def _cmp2_kernel(ab_ref, pe_ref, w1_ref, w2_ref, qn_ref, ck_ref, cv_ref, *, nc):
    for c in range(2):
        bias = _dot(pe_ref[c], w1_ref[c])[0:1, :]
        ab = ab_ref[c, 0]
        pre = ab[:, :HEAD_DIM] + pltpu.roll(ab[:, HEAD_DIM:], nc - 1, 0) + bias
        out = _dot(_silu(pre).astype(BF), w2_ref[c])
        if c == 0:
            ck_ref[0, 0] = _rms(out, qn_ref[1:2, :]).astype(BF)
        else:
            cv_ref[0, 0] = out.astype(BF)


def _cmp_stage2(ab, nb, pe_flat, w1_flat, w2_bf, qk_norm):
    nc = ab.shape[2] // nb
    shp = jax.ShapeDtypeStruct((nb, KV_HEADS, nc, HEAD_DIM), BF)
    kdim = CMP_LEN * HEAD_DIM
    return pl.pallas_call(
        functools.partial(_cmp2_kernel, nc=nc),
        grid=(nb, KV_HEADS),
        in_specs=[pl.BlockSpec((2, 1, nc, 2 * HEAD_DIM), lambda b, g: (0, g, b, 0)),
                  pl.BlockSpec((2, SUB, kdim), lambda b, g: (0, 0, 0)),
                  pl.BlockSpec((2, kdim, HEAD_DIM), lambda b, g: (0, 0, 0)),
                  pl.BlockSpec((2, HEAD_DIM, HEAD_DIM), lambda b, g: (0, 0, 0)),
                  pl.BlockSpec((4, HEAD_DIM), lambda b, g: (0, 0))],
        out_specs=[pl.BlockSpec((1, 1, nc, HEAD_DIM), lambda b, g: (b, g, 0, 0))] * 2,
        out_shape=[shp, shp],
        compiler_params=_params("parallel", "parallel"),
    )(ab, pe_flat, w1_flat, w2_bf, qk_norm)


def _kvprep_kernel(s_ref, w_ref, qn_ref, so_ref, wo_ref):
    half = KV_HEADS * HEAD_DIM
    for src, dst, row in ((s_ref, so_ref, 2), (w_ref, wo_ref, 3)):
        gain = qn_ref[row:row + 1, :]
        for g in range(KV_HEADS):
            sl = slice(g * HEAD_DIM, (g + 1) * HEAD_DIM)
            dst[:, sl] = _rms(src[:, sl], gain).astype(BF)
        dst[:, half:] = src[:, half:].astype(BF)


def _kv_prep(z, qk_norm):
    M = z.shape[0]
    tb = 512
    shp = jax.ShapeDtypeStruct((M, NSA_KV), BF)
    return pl.pallas_call(
        _kvprep_kernel,
        grid=(M // tb,),
        in_specs=[pl.BlockSpec((tb, NSA_KV), lambda i: (i, 3)),
                  pl.BlockSpec((tb, NSA_KV), lambda i: (i, 4)),
                  pl.BlockSpec((4, HEAD_DIM), lambda i: (0, 0))],
        out_specs=[pl.BlockSpec((tb, NSA_KV), lambda i: (i, 0))] * 2,
        out_shape=[shp, shp],
        compiler_params=_params("parallel"),
    )(z, z, qk_norm)


def _block_rank(val, lane, n_blocks):
    rank = jnp.zeros(val.shape, jnp.int32)
    for jp in range(n_blocks):
        col = val[:, jp:jp + 1]
        rank = rank + ((col > val) | ((col == val) & (lane > jp))).astype(jnp.int32)
    return rank


def _gate_col(bg, lane, idx):
    return jnp.sum(jnp.where(lane == idx, bg, 0.0), axis=-1, keepdims=True)


def _nsa_attn_kernel(q_ref, gt_ref, bg_ref, ck_ref, cv_ref, sk_ref, sv_ref, wk_ref, wv_ref, qn_ref, smap_ref, e_ref,
                     o_ref, acc_ref, m_ref, l_ref, *, qb, n_slc):
    g = pl.program_id(1)
    i = pl.program_id(2)
    d = HEAD_DIM
    rows = GQA * qb
    gain = qn_ref[0:1, :] * (d ** -0.5)
    q = jnp.concatenate([_rms(q_ref[:, r * d:(r + 1) * d], gain).astype(BF) for r in range(GQA)], axis=0)
    qpos = i * qb + (lax.broadcasted_iota(jnp.int32, (rows, 1), 0) & (qb - 1))

    ncp = ck_ref.shape[2]
    cend = lax.broadcasted_iota(jnp.int32, (1, ncp), 1) * CMP_STRIDE + (CMP_LEN - 1)
    pc = _masked_softmax(_dot_nt(q, ck_ref[0, 0]), cend <= qpos)
    oc = _dot(pc.astype(BF), cv_ref[0, 0])
    pcs = pc[0:qb]
    for r in range(1, GQA):
        pcs = pcs + pc[r * qb:(r + 1) * qb]
    hi, lo = _split_bf(pcs)
    imp = _dot(hi, smap_ref[...]) + _dot(lo, smap_ref[...])
    lane = lax.broadcasted_iota(jnp.int32, (qb, LANES), 1)
    cur = jnp.right_shift(i * qb + lax.broadcasted_iota(jnp.int32, (qb, 1), 0), int(math.log2(SLC_BLOCK)))
    vis = lane <= cur
    forced = vis & ((lane == 0) | (lane >= cur - 1))
    val = jnp.where(forced, BIG, jnp.where(vis, imp, -BIG))
    sel = ((_block_rank(val, lane, n_slc) < N_SELECT) & vis).astype(BF)

    def flash(k_ref, v_ref, lo_t, hi_t, mask_fn):
        m_ref[...] = jnp.full(m_ref.shape, NEG, F32)
        l_ref[...] = jnp.zeros(l_ref.shape, F32)
        acc_ref[...] = jnp.zeros(acc_ref.shape, F32)

        def body(t, carry):
            start = pl.multiple_of(t * qb, qb)
            s = _dot_nt(q, k_ref[pl.ds(start, qb), :])
            kpos = t * qb + lax.broadcasted_iota(jnp.int32, (1, qb), 1)
            mask = mask_fn(start, kpos)
            m_prev = m_ref[...]
            m_new = jnp.maximum(m_prev, jnp.max(jnp.where(mask, s, NEG), axis=-1, keepdims=True))
            p = jnp.where(mask, jnp.exp(s - m_new), 0.0)
            alpha = jnp.exp(m_prev - m_new)
            l_ref[...] = alpha * l_ref[...] + jnp.sum(p, axis=-1, keepdims=True)
            acc_ref[...] = alpha * acc_ref[...] + _dot(p.astype(BF), v_ref[pl.ds(start, qb), :])
            m_ref[...] = m_new
            return carry

        lax.fori_loop(lo_t, hi_t, body, 0)
        l = l_ref[...]
        return acc_ref[...] / jnp.where(l > 0, l, 1.0)

    def sel_mask(start, kpos):
        hit = _dot(sel, e_ref[:, pl.ds(start, qb)])
        return (jnp.concatenate([hit] * GQA, axis=0) > 0.5) & (kpos <= qpos)

    def win_mask(start, kpos):
        return (kpos <= qpos) & (kpos >= qpos - WINDOW)

    osel = flash(sk_ref, sv_ref, 0, i + 1, sel_mask)
    owin = flash(wk_ref, wv_ref, jnp.maximum(i - WINDOW // qb, 0), i + 1, win_mask)

    bg = jax.nn.sigmoid(bg_ref[...])
    for r in range(GQA):
        rs = slice(r * qb, (r + 1) * qb)
        base = g * (GQA * 3) + r * 3
        o = (_gate_col(bg, lane, base) * oc[rs] + _gate_col(bg, lane, base + 1) * osel[rs]
             + _gate_col(bg, lane, base + 2) * owin[rs])
        o_ref[:, r * d:(r + 1) * d] = (o * _silu(gt_ref[:, r * d:(r + 1) * d])).astype(BF)


def _selection_map(n_cmp, n_slc, rows, cols):
    i = np.arange(n_cmp)[:, None]
    j = np.arange(n_slc)[None, :]
    lo = np.maximum(i * CMP_STRIDE, j * SLC_BLOCK)
    hi = np.minimum(i * CMP_STRIDE + CMP_LEN, (j + 1) * SLC_BLOCK)
    out = np.zeros((rows, cols), np.float32)
    out[:n_cmp, :n_slc] = np.maximum(hi - lo, 0) / CMP_STRIDE
    return jnp.asarray(out, BF)


def _nsa_attn_prompt(z, zbg, ck, cv, sk, wk, qk_norm, B, T):
    qb = 128
    nt = T // qb
    n_slc = T // SLC_BLOCK
    assert n_slc <= LANES
    ncp = ck.shape[2]
    smap = _selection_map(ncp - 1, n_slc, ncp, LANES)
    emat = np.zeros((LANES, T), np.float32)
    emat[np.arange(T) // SLC_BLOCK, np.arange(T)] = 1.0
    wq = GQA * HEAD_DIM
    gate0 = (D_INNER + 3 * NSA_KV) // wq
    rows = GQA * qb
    kv_spec = lambda col0: pl.BlockSpec((T, HEAD_DIM), lambda b, g, i: (b, col0 + g))
    return pl.pallas_call(
        functools.partial(_nsa_attn_kernel, qb=qb, n_slc=n_slc),
        grid=(B, KV_HEADS, nt),
        in_specs=[pl.BlockSpec((qb, wq), lambda b, g, i: (b * nt + i, g)),
                  pl.BlockSpec((qb, wq), lambda b, g, i: (b * nt + i, gate0 + g)),
                  pl.BlockSpec((qb, LANES), lambda b, g, i: (b * nt + i, 0)),
                  pl.BlockSpec((1, 1, ncp, HEAD_DIM), lambda b, g, i: (b, g, 0, 0)),
                  pl.BlockSpec((1, 1, ncp, HEAD_DIM), lambda b, g, i: (b, g, 0, 0)),
                  kv_spec(0), kv_spec(KV_HEADS), kv_spec(0), kv_spec(KV_HEADS),
                  pl.BlockSpec((4, HEAD_DIM), lambda b, g, i: (0, 0)),
                  pl.BlockSpec((ncp, LANES), lambda b, g, i: (0, 0)),
                  pl.BlockSpec((LANES, T), lambda b, g, i: (0, 0))],
        out_specs=pl.BlockSpec((qb, wq), lambda b, g, i: (b * nt + i, g)),
        out_shape=jax.ShapeDtypeStruct((B * T, D_INNER), BF),
        scratch_shapes=[pltpu.VMEM((rows, HEAD_DIM), F32), pltpu.VMEM((rows, 1), F32), pltpu.VMEM((rows, 1), F32)],
        compiler_params=_params("parallel", "parallel", "arbitrary"),
    )(z, z, zbg, ck, cv, sk, sk, wk, wk, qk_norm, smap, jnp.asarray(emat, BF))


def _dec_cmp_kernel(q_ref, ck_ref, cv_ref, qn_ref, smap_ref, tri_ref, oc_ref, idx_ref, *, n_slc):
    d = HEAD_DIM
    nc = ck_ref.shape[2]
    q = _rms(q_ref[0, 0], qn_ref[0:1, :] * (d ** -0.5)).astype(BF)
    ncol = lax.broadcasted_iota(jnp.int32, (1, nc), 1)
    pc = _masked_softmax(_dot_nt(q, ck_ref[0, 0]), ncol <= nc - 2)
    oc_ref[0, 0] = _dot(pc.astype(BF), cv_ref[0, 0])
    row = lax.broadcasted_iota(jnp.int32, (SUB, 1), 0)
    pcs = jnp.sum(jnp.where(row < GQA, pc, 0.0), axis=0, keepdims=True)
    hi, lo = _split_bf(jnp.broadcast_to(pcs, (SUB, nc)))
    imp = _dot(hi, smap_ref[...]) + _dot(lo, smap_ref[...])
    width = smap_ref.shape[1]
    lane = lax.broadcasted_iota(jnp.int32, (SUB, width), 1)
    cur = n_slc - 1
    vis = lane <= cur
    forced = (lane == 0) | ((lane >= cur - 1) & vis)
    val = jnp.where(forced, BIG, jnp.where(vis, imp, -BIG))
    sel = (_block_rank(val, lane, n_slc) < N_SELECT) & (lane < cur)
    cnt = _dot(sel.astype(BF), tri_ref[...])
    lane_f = lane.astype(F32)
    out_lane = lax.broadcasted_iota(jnp.int32, (SUB, LANES), 1)
    idx = jnp.zeros((SUB, LANES), F32)
    for k in range(N_SELECT - 1):
        pick = sel & (jnp.abs(cnt - (k + 1.0)) < 0.5)
        idx = idx + jnp.where(out_lane == k, jnp.sum(jnp.where(pick, lane_f, 0.0), axis=-1, keepdims=True), 0.0)
    idx_ref[0, 0] = idx.astype(jnp.int32)


def _dec_cmp(q4, ck, cv, qk_norm, n_slc):
    DB, G, nc, d = ck.shape
    width = 2 * LANES
    assert n_slc <= width
    smap = _selection_map(nc - 1, n_slc, nc, width)
    tri = jnp.asarray(np.triu(np.ones((width, width), np.float32)), BF)
    blk = lambda b, g: (b, g, 0, 0)
    return pl.pallas_call(
        functools.partial(_dec_cmp_kernel, n_slc=n_slc),
        grid=(DB, G),
        in_specs=[pl.BlockSpec((1, 1, SUB, d), blk),
                  pl.BlockSpec((1, 1, nc, d), blk),
                  pl.BlockSpec((1, 1, nc, d), blk),
                  pl.BlockSpec((4, d), lambda b, g: (0, 0)),
                  pl.BlockSpec((nc, width), lambda b, g: (0, 0)),
                  pl.BlockSpec((width, width), lambda b, g: (0, 0))],
        out_specs=[pl.BlockSpec((1, 1, SUB, d), blk), pl.BlockSpec((1, 1, SUB, LANES), blk)],
        out_shape=[jax.ShapeDtypeStruct((DB, G, SUB, d), F32), jax.ShapeDtypeStruct((DB, G, SUB, LANES), jnp.int32)],
        compiler_params=_params("parallel", "parallel"),
    )(q4, ck, cv, qk_norm, smap, tri)


def _dec_sel_kernel(tbl_ref, *refs):
    del tbl_ref
    blocks = refs[:KV_HEADS]
    q_ref, z_ref, qn_ref, o_ref, acc_ref, m_ref, l_ref = refs[KV_HEADS:]
    d = HEAD_DIM
    s_id = pl.program_id(1)
    last = pl.num_programs(1) - 1

    @pl.when(s_id == 0)
    def _():
        m_ref[...] = jnp.full(m_ref.shape, NEG, F32)
        l_ref[...] = jnp.zeros(l_ref.shape, F32)
        acc_ref[...] = jnp.zeros(acc_ref.shape, F32)

    qgain = qn_ref[0:1, :] * (d ** -0.5)
    kgain = qn_ref[2:3, :]

    def update(g, s, mask, v):
        m_prev = m_ref[g]
        m_new = jnp.maximum(m_prev, jnp.max(jnp.where(mask, s, NEG), axis=-1, keepdims=True))
        p = jnp.where(mask, jnp.exp(s - m_new), 0.0)
        alpha = jnp.exp(m_prev - m_new)
        l_ref[g] = alpha * l_ref[g] + jnp.sum(p, axis=-1, keepdims=True)
        acc_ref[g] = alpha * acc_ref[g] + _dot(p.astype(BF), v)
        m_ref[g] = m_new

    for g in range(KV_HEADS):
        q = _rms(q_ref[0, g], qgain).astype(BF)
        k = _rms(blocks[g][pl.ds(g, SLC_BLOCK, stride=KV_ROWS), :], kgain).astype(BF)
        v = blocks[g][pl.ds(KV_HEADS + g, SLC_BLOCK, stride=KV_ROWS), :].astype(BF)
        update(g, _dot_nt(q, k), jnp.full((SUB, SLC_BLOCK), True), v)

    @pl.when(s_id == last)
    def _():
        first = lax.broadcasted_iota(jnp.int32, (SUB, SUB), 1) == 0
        half = KV_HEADS * d
        for g in range(KV_HEADS):
            q = _rms(q_ref[0, g], qgain).astype(BF)
            k = _rms(z_ref[0, :, g * d:(g + 1) * d], kgain).astype(BF)
            update(g, _dot_nt(q, k), first, z_ref[0, :, half + g * d:half + (g + 1) * d].astype(BF))
            o_ref[0, g] = acc_ref[g] / l_ref[g]


def _dec_sel(tbl, slc_blocks, q4, zpad, qk_norm, n_blk):
    DB, G, _, d = q4.shape
    kv_specs = [pl.BlockSpec((None, SLC_BLOCK * KV_ROWS, d), lambda b, s, t, g=g: (t[(b * G + g) * n_blk + s], 0, 0))
                for g in range(G)]
    return pl.pallas_call(
        _dec_sel_kernel,
        grid_spec=pltpu.PrefetchScalarGridSpec(
            num_scalar_prefetch=1,
            grid=(DB, n_blk),
            in_specs=kv_specs + [pl.BlockSpec((1, G, SUB, d), lambda b, s, t: (b, 0, 0, 0)),
                                 pl.BlockSpec((1, SUB, NSA_KV), lambda b, s, t: (b, 0, 3)),
                                 pl.BlockSpec((4, d), lambda b, s, t: (0, 0))],
            out_specs=pl.BlockSpec((1, G, SUB, d), lambda b, s, t: (b, 0, 0, 0)),
            scratch_shapes=[pltpu.VMEM((G, SUB, d), F32), pltpu.VMEM((G, SUB, 1), F32), pltpu.VMEM((G, SUB, 1), F32)],
        ),
        out_shape=jax.ShapeDtypeStruct((DB, G, SUB, d), F32),
        compiler_params=_params("parallel", "arbitrary"),
    )(tbl, *([slc_blocks] * G), q4, zpad, qk_norm)


def _dec_win_kernel(win_ref, q_ref, z_ref, bg_ref, oc_ref, os_ref, qn_ref, o_ref):
    d = HEAD_DIM
    wlen = win_ref.shape[0] // KV_ROWS
    half = KV_HEADS * d
    kw0 = D_INNER + 2 * NSA_KV
    gate0 = D_INNER + 3 * NSA_KV
    qgain = qn_ref[0:1, :] * (d ** -0.5)
    kgain = qn_ref[3:4, :]
    first = lax.broadcasted_iota(jnp.int32, (SUB, SUB), 1) == 0
    lane = lax.broadcasted_iota(jnp.int32, (1, LANES), 1)
    bg = jax.nn.sigmoid(bg_ref[0])
    o_ref[...] = jnp.zeros(o_ref.shape, BF)
    for g in range(KV_HEADS):
        q = _rms(q_ref[0, g], qgain).astype(BF)
        k = _rms(win_ref[pl.ds(g, wlen, stride=KV_ROWS), :], kgain).astype(BF)
        v = win_ref[pl.ds(KV_HEADS + g, wlen, stride=KV_ROWS), :].astype(BF)
        kn = _rms(z_ref[0, :, kw0 + g * d:kw0 + (g + 1) * d], kgain).astype(BF)
        vn = z_ref[0, :, kw0 + half + g * d:kw0 + half + (g + 1) * d].astype(BF)
        s = _dot_nt(q, k)
        sn = jnp.where(first, _dot_nt(q, kn), NEG)
        m = jnp.maximum(jnp.max(s, axis=-1, keepdims=True), jnp.max(sn, axis=-1, keepdims=True))
        p = jnp.exp(s - m)
        pn = jnp.where(first, jnp.exp(sn - m), 0.0)
        den = jnp.sum(p, axis=-1, keepdims=True) + jnp.sum(pn, axis=-1, keepdims=True)
        ow = (_dot(p.astype(BF), v) + _dot(pn.astype(BF), vn)) / den
        oc, os_ = oc_ref[0, g], os_ref[0, g]
        for r in range(GQA):
            base = g * (GQA * 3) + r * 3
            o = (_gate_col(bg, lane, base) * oc[r:r + 1] + _gate_col(bg, lane, base + 1) * os_[r:r + 1]
                 + _gate_col(bg, lane, base + 2) * ow[r:r + 1])
            col = (g * GQA + r) * d
            o_ref[0, 0:1, col:col + d] = (o * _silu(z_ref[0, 0:1, gate0 + col:gate0 + col + d])).astype(BF)


def _dec_win(win, q4, zpad, zbg, oc, os_, qk_norm):
    DB, G, _, d = q4.shape
    wrows = win.shape[1]
    N = zpad.shape[2]
    blk4 = pl.BlockSpec((1, G, SUB, d), lambda b: (b, 0, 0, 0))
    return pl.pallas_call(
        _dec_win_kernel,
        grid=(DB,),
        in_specs=[pl.BlockSpec((None, wrows, d), lambda b: (b, 0, 0)),
                  blk4,
                  pl.BlockSpec((1, SUB, N), lambda b: (b, 0, 0)),
                  pl.BlockSpec((1, 1, LANES), lambda b: (b, 0, 0)),
                  blk4, blk4,
                  pl.BlockSpec((4, d), lambda b: (0, 0))],
        out_specs=pl.BlockSpec((1, SUB, D_INNER), lambda b: (b, 0, 0)),
        out_shape=jax.ShapeDtypeStruct((DB, SUB, D_INNER), BF),
        compiler_params=_params("parallel"),
    )(win, q4, zpad, zbg[:, None, :], oc, os_, qk_norm)


def _nsa_layer(xp, xs, norm, w_in, qk_norm, cmp_pe, cmp_w1, cmp_w2, w_out, cache_cmp, cache_slc, win, page_table):
    B, T, D = xp.shape
    DB = xs.shape[0]
    G, d = KV_HEADS, HEAD_DIM
    n_main = 2 * D_INNER + 3 * NSA_KV
    w_main = w_in[:, :n_main].astype(BF)
    w_tail = _pad_cols(w_in[:, n_main:], LANES).astype(BF)
    w_out_bf = w_out.astype(BF)
    xp2, xs2 = xp.reshape(B * T, D), xs.reshape(DB, D)
    zp, zs = _norm_proj(xp2, norm, w_main, 512), _norm_proj(xs2, norm, w_main, 512)
    bgp, bgs = _norm_proj(xp2, norm, w_tail, LANES), _norm_proj(xs2, norm, w_tail, LANES)

    w1cat = jnp.concatenate([cmp_w1[:, :CMP_STRIDE], cmp_w1[:, CMP_STRIDE:]], axis=-1).astype(BF)
    w1_flat = cmp_w1.reshape(2, CMP_LEN * d, d).astype(BF)
    pe_flat = jnp.broadcast_to(cmp_pe.transpose(1, 0, 2).reshape(2, 1, CMP_LEN * d), (2, SUB, CMP_LEN * d)).astype(BF)
    w2_bf = cmp_w2.astype(BF)

    n_pages = (B * T) // PAGE_SIZE
    kvp = zp[:, D_INNER:D_INNER + 3 * NSA_KV].reshape(B, T, 3, 2, G, d)
    win_p = kvp[:, max(0, T - WINDOW):, 2]
    ab = _cmp_stage1(kvp[:, :, 0].reshape(n_pages, PAGE_SIZE * KV_ROWS, d), jnp.arange(n_pages, dtype=jnp.int32), w1cat)
    ckp, cvp = _cmp_stage2(ab, B, pe_flat, w1_flat, w2_bf, qk_norm)
    skp, wkp = _kv_prep(zp, qk_norm)
    op = _nsa_attn_prompt(zp, bgp, ckp, cvp, skp, wkp, qk_norm, B, T)

    past = page_table.shape[1] * PAGE_SIZE
    n_slc = past // SLC_BLOCK + 1
    n_pool = cache_cmp.shape[0]
    ab = _cmp_stage1(cache_cmp.reshape(n_pool, PAGE_SIZE * KV_ROWS, d), page_table.reshape(-1), w1cat)
    cks, cvs = _cmp_stage2(ab, DB, pe_flat, w1_flat, w2_bf, qk_norm)
    q4 = jnp.pad(zs[:, :D_INNER].reshape(DB, G, GQA, d), ((0, 0), (0, 0), (0, SUB - GQA), (0, 0)))
    zpad = jnp.pad(zs[:, None, :], ((0, 0), (0, SUB - 1), (0, 0)))
    oc, idx = _dec_cmp(q4, cks, cvs, qk_norm, n_slc)
    idx = idx[:, :, 0, :N_SELECT - 1]
    per_page = PAGE_SIZE // SLC_BLOCK
    phys = jnp.take_along_axis(page_table, (idx // per_page).reshape(DB, -1), axis=1).reshape(idx.shape)
    tbl = (phys * per_page + idx % per_page).reshape(-1).astype(jnp.int32)
    os_ = _dec_sel(tbl, cache_slc.reshape(n_pool * per_page, SLC_BLOCK * KV_ROWS, d), q4, zpad, qk_norm, N_SELECT - 1)
    osm = _dec_win(win.reshape(DB, win.shape[1] * KV_ROWS, d), q4, zpad, bgs, oc, os_, qk_norm)[:, 0, :]
    kvs = zs[:, D_INNER:D_INNER + 3 * NSA_KV].reshape(DB, 1, 3, 2, G, d)
    win_s = jnp.concatenate([win[:, 1:], kvs[:, :, 2]], axis=1)

    xp = _out_proj(op, w_out_bf, xp2).reshape(B, T, D)
    xs = _out_proj(osm, w_out_bf, xs2).reshape(DB, 1, D)
    return xp, xs, kvp[:, :, 0], kvs[:, :, 0], kvp[:, :, 1], kvs[:, :, 1], win_p, win_s


def kernel(x_prompt, x_sample, state_l0_ret, cache_l1_cmp, cache_l1_slc, state_l1_win, state_l2_conv, state_l2_gdn, state_l3_ret, page_table, l0_norm, l0_w_in, l0_gn, l0_w_out, l1_norm, l1_w_in, l1_qk_norm, l1_cmp_pe, l1_cmp_w1, l1_cmp_w2, l1_w_out, l2_norm, l2_w_in, l2_conv_w, l2_A_log, l2_dt_bias, l2_onorm, l2_w_out, l3_norm, l3_w_in, l3_gn, l3_w_out):
    xp, xs, r0p, r0s = _retention_layer(x_prompt, x_sample, l0_norm, l0_w_in, l0_gn, l0_w_out, state_l0_ret)
    xp, xs, cp, cs, slp, sls, wp, ws = _nsa_layer(xp, xs, l1_norm, l1_w_in, l1_qk_norm, l1_cmp_pe, l1_cmp_w1, l1_cmp_w2,
                                                  l1_w_out, cache_l1_cmp, cache_l1_slc, state_l1_win, page_table)
    xp, xs, cvp, cvs, gp, gs = _gdn_layer(xp, xs, l2_norm, l2_w_in, l2_conv_w, l2_A_log, l2_dt_bias, l2_onorm, l2_w_out,
                                          state_l2_conv, state_l2_gdn)
    xp, xs, r3p, r3s = _retention_layer(xp, xs, l3_norm, l3_w_in, l3_gn, l3_w_out, state_l3_ret)
    return (xp, xs, r0p, r0s, cp, cs, slp, sls, wp, ws, cvp, cvs, gp, gs, r3p, r3s)
```

```python
import functools
import math

import jax
import jax.numpy as jnp
import numpy as np
from jax import lax
from jax.experimental import pallas as pl
from jax.experimental.pallas import tpu as pltpu

F32 = jnp.float32
BF = jnp.bfloat16
EPS = 1e-6
LANES = 128
VMEM_LIMIT = 56 * 2**20

D_MODEL = 1024
D_INNER = 2 * D_MODEL
PAST_LEN = 8192
PAGE_SIZE = 128
R_HEADS, R_DK, R_DV = 8, 128, 256
ROPE_BASE = 10000.0
N_HEADS, KV_HEADS, HEAD_DIM = 16, 4, 128
CMP_LEN, CMP_STRIDE, SLC_BLOCK, N_SELECT, WINDOW = 32, 16, 64, 16, 512
NSA_KV = 2 * KV_HEADS * HEAD_DIM
G_HEADS, G_DK, G_DV, CONV_W, G_CHUNK = 16, 128, 128, 4, 64


def _params(*sem):
    return pltpu.CompilerParams(dimension_semantics=sem, vmem_limit_bytes=VMEM_LIMIT)


def _dot(a, b):
    return jnp.dot(a, b, preferred_element_type=F32)


def _dot_nt(a, b):
    return lax.dot_general(a, b, (((1,), (1,)), ((), ())), preferred_element_type=F32)


def _dot_tn(a, b):
    return lax.dot_general(a, b, (((0,), (0,)), ((), ())), preferred_element_type=F32)


def _silu(x):
    return x * jax.nn.sigmoid(x)


def _proj_kernel(x_ref, g_ref, w_ref, o_ref, xn_ref):
    @pl.when(pl.program_id(1) == 0)
    def _():
        x = x_ref[...]
        ms = jnp.mean(x * x, axis=-1, keepdims=True)
        xn_ref[...] = (x * lax.rsqrt(ms + EPS) * g_ref[...]).astype(BF)

    o_ref[...] = _dot(xn_ref[...], w_ref[...])


def _norm_proj(x, gain, w_bf, tn):
    M, D = x.shape
    N = w_bf.shape[1]
    tm = min(M, 1024)
    return pl.pallas_call(
        _proj_kernel,
        name="norm_proj",
        grid=(M // tm, N // tn),
        in_specs=[pl.BlockSpec((tm, D), lambda i, j: (i, 0)),
                  pl.BlockSpec((1, D), lambda i, j: (0, 0)),
                  pl.BlockSpec((D, tn), lambda i, j: (0, j))],
        out_specs=pl.BlockSpec((tm, tn), lambda i, j: (i, j)),
        out_shape=jax.ShapeDtypeStruct((M, N), F32),
        scratch_shapes=[pltpu.VMEM((tm, D), BF)],
        compiler_params=_params("parallel", "arbitrary"),
    )(x, gain.reshape(1, D), w_bf)


def _out_kernel(a_ref, w_ref, x_ref, o_ref):
    o_ref[...] = x_ref[...] + _dot(a_ref[...], w_ref[...])


def _out_proj(a_bf, w_bf, x):
    M, K = a_bf.shape
    N = w_bf.shape[1]
    tm = min(M, 1024)
    tn = 512
    return pl.pallas_call(
        _out_kernel,
        name="out_proj",
        grid=(M // tm, N // tn),
        in_specs=[pl.BlockSpec((tm, K), lambda i, j: (i, 0)),
                  pl.BlockSpec((K, tn), lambda i, j: (0, j)),
                  pl.BlockSpec((tm, tn), lambda i, j: (i, j))],
        out_specs=pl.BlockSpec((tm, tn), lambda i, j: (i, j)),
        out_shape=jax.ShapeDtypeStruct((M, N), F32),
        compiler_params=_params("parallel", "arbitrary"),
    )(a_bf, w_bf, x)


def _rope_tables(pos):
    half = R_DK // 2
    inv = ROPE_BASE ** (-jnp.arange(half, dtype=F32) / half)
    ang = pos.astype(F32)[:, None] * inv[None, :]
    cos, sin = jnp.cos(ang), jnp.sin(ang)
    return jnp.concatenate([cos, cos], axis=-1), jnp.concatenate([-sin, sin], axis=-1)


def _log_gamma():
    return jnp.log1p(-jnp.exp2(-5.0 - jnp.arange(R_HEADS, dtype=F32)))


def _group_norm_gate(o, gn, gate):
    mu = jnp.mean(o, axis=-1, keepdims=True)
    xc = o - mu
    var = jnp.mean(xc * xc, axis=-1, keepdims=True)
    return (xc * lax.rsqrt(var + EPS) * gn * _silu(gate)).astype(BF)


def _ret_kernel(lg_ref, q_ref, k_ref, v_ref, gt_ref, cos_ref, sin_ref, gn_ref, o_ref, s_ref, *, c, nchunk):
    h = pl.program_id(1)
    lg = lg_ref[h]

    @pl.when(pl.program_id(2) == 0)
    def _():
        s_ref[...] = jnp.zeros_like(s_ref)

    ii = lax.broadcasted_iota(jnp.int32, (c, c), 0)
    jj = lax.broadcasted_iota(jnp.int32, (c, c), 1)
    dec = jnp.where(ii >= jj, jnp.exp((ii - jj).astype(F32) * lg), 0.0)
    ridx = lax.broadcasted_iota(jnp.int32, (c, 1), 0).astype(F32)
    eg = jnp.exp((ridx + 1.0) * lg)
    ek = jnp.exp((c - 1.0 - ridx) * lg)
    egl = jnp.exp(jnp.full((1, 1), c, F32) * lg)
    gn = gn_ref[...]
    for ci in range(nchunk):
        sl = pl.ds(ci * c, c)
        cos, sin = cos_ref[sl, :], sin_ref[sl, :]
        q, k = q_ref[sl, :], k_ref[sl, :]
        qr = q * cos + pltpu.roll(q, R_DK // 2, 1) * sin
        kr = (k * cos + pltpu.roll(k, R_DK // 2, 1) * sin) * (R_DK ** -0.5)
        v = v_ref[sl, :].astype(BF)
        s = s_ref[0, 0]
        att = _dot_nt(qr.astype(BF), kr.astype(BF)) * dec
        o = _dot(att.astype(BF), v) + _dot((qr * eg).astype(BF), s.astype(BF))
        s_ref[0, 0] = s * egl + _dot_tn((kr * ek).astype(BF), v)
        o_ref[sl, :] = _group_norm_gate(o, gn, gt_ref[sl, :])


def _retention_prompt(z, B, T, gn):
    c = 256
    tb = 512
    nt = T // tb
    cos, sin = _rope_tables(jnp.arange(T))
    kq, kv = R_HEADS, (2 * R_HEADS * R_DK) // R_DV
    return pl.pallas_call(
        functools.partial(_ret_kernel, c=c, nchunk=tb // c),
        name="ret_chunks",
        grid=(B, R_HEADS, nt),
        in_specs=[pl.BlockSpec(memory_space=pltpu.SMEM),
                  pl.BlockSpec((tb, R_DK), lambda b, h, i: (b * nt + i, h)),
                  pl.BlockSpec((tb, R_DK), lambda b, h, i: (b * nt + i, kq + h)),
                  pl.BlockSpec((tb, R_DV), lambda b, h, i: (b * nt + i, kv + h)),
                  pl.BlockSpec((tb, R_DV), lambda b, h, i: (b * nt + i, kv + R_HEADS + h)),
                  pl.BlockSpec((tb, R_DK), lambda b, h, i: (i, 0)),
                  pl.BlockSpec((tb, R_DK), lambda b, h, i: (i, 0)),
                  pl.BlockSpec((1, R_DV), lambda b, h, i: (0, h))],
        out_specs=[pl.BlockSpec((tb, R_DV), lambda b, h, i: (b * nt + i, h)),
                   pl.BlockSpec((1, 1, R_DK, R_DV), lambda b, h, i: (b, h, 0, 0))],
        out_shape=[jax.ShapeDtypeStruct((B * T, D_INNER), BF),
                   jax.ShapeDtypeStruct((B, R_HEADS, R_DK, R_DV), F32)],
        compiler_params=_params("parallel", "parallel", "arbitrary"),
    )(_log_gamma(), z, z, z, z, cos, sin, gn.reshape(1, D_INNER))


SUB = 8


def _ret_step_kernel(lg_ref, z_ref, s0_ref, cos_ref, sin_ref, gn_ref, o_ref, s_ref):
    cos, sin = cos_ref[...], sin_ref[...]
    qk = R_HEADS * R_DK
    for h in range(R_HEADS):
        gam = jnp.exp(jnp.full((1, 1), 1.0, F32) * lg_ref[h])
        q = z_ref[0, :, h * R_DK:(h + 1) * R_DK]
        k = z_ref[0, :, qk + h * R_DK:qk + (h + 1) * R_DK]
        v = z_ref[0, :, 2 * qk + h * R_DV:2 * qk + (h + 1) * R_DV]
        gate = z_ref[0, :, 2 * qk + D_INNER + h * R_DV:2 * qk + D_INNER + (h + 1) * R_DV]
        qr = q * cos + pltpu.roll(q, R_DK // 2, 1) * sin
        kr = (k * cos + pltpu.roll(k, R_DK // 2, 1) * sin) * (R_DK ** -0.5)
        s = s0_ref[0, h]
        att = jnp.sum(qr * kr, axis=-1, keepdims=True)
        o = att * v + _dot((qr * gam).astype(BF), s.astype(BF))
        s_ref[0, h] = s * gam + _dot_tn(kr.astype(BF), v.astype(BF))
        o_ref[0, :, h * R_DV:(h + 1) * R_DV] = _group_norm_gate(o, gn_ref[:, h * R_DV:(h + 1) * R_DV], gate)


def _retention_step(z, s0, gn):
    DB, N = z.shape
    zp = jnp.pad(z[:, None, :], ((0, 0), (0, SUB - 1), (0, 0)))
    cos, sin = _rope_tables(jnp.full((1,), PAST_LEN))
    o, s = pl.pallas_call(
        _ret_step_kernel,
        name="ret_step",
        grid=(DB,),
        in_specs=[pl.BlockSpec(memory_space=pltpu.SMEM),
                  pl.BlockSpec((1, SUB, N), lambda b: (b, 0, 0)),
                  pl.BlockSpec((1, R_HEADS, R_DK, R_DV), lambda b: (b, 0, 0, 0)),
                  pl.BlockSpec((1, R_DK), lambda b: (0, 0)),
                  pl.BlockSpec((1, R_DK), lambda b: (0, 0)),
                  pl.BlockSpec((1, D_INNER), lambda b: (0, 0))],
        out_specs=[pl.BlockSpec((1, SUB, D_INNER), lambda b: (b, 0, 0)),
                   pl.BlockSpec((1, R_HEADS, R_DK, R_DV), lambda b: (b, 0, 0, 0))],
        out_shape=[jax.ShapeDtypeStruct((DB, SUB, D_INNER), BF),
                   jax.ShapeDtypeStruct(s0.shape, F32)],
        compiler_params=_params("parallel"),
    )(_log_gamma(), zp, s0, cos, sin, gn.reshape(1, D_INNER))
    return o[:, 0, :], s


def _retention_layer(xp, xs, norm, w_in, gn, w_out, s0):
    B, T, D = xp.shape
    DB = xs.shape[0]
    w_in_bf, w_out_bf = w_in.astype(BF), w_out.astype(BF)
    zp = _norm_proj(xp.reshape(B * T, D), norm, w_in_bf, 512)
    zs = _norm_proj(xs.reshape(DB, D), norm, w_in_bf, 512)
    op, sp = _retention_prompt(zp, B, T, gn)
    os_, ss = _retention_step(zs, s0, gn)
    xp = _out_proj(op, w_out_bf, xp.reshape(B * T, D)).reshape(B, T, D)
    xs = _out_proj(os_, w_out_bf, xs.reshape(DB, D)).reshape(DB, 1, D)
    return xp, xs, sp, ss


def _conv_kernel(x_ref, halo_ref, w_ref, o_ref, ext_ref, *, tb):
    sec = pl.program_id(2)
    first = pl.program_id(1) == 0
    ext_ref[pl.ds(SUB, tb), :] = x_ref[...]
    ext_ref[pl.ds(0, SUB), :] = jnp.where(first, 0.0, halo_ref[...])
    conv = ext_ref[pl.ds(SUB - 3, tb), :] * w_ref[0:1, :]
    for j in range(1, CONV_W):
        conv = conv + ext_ref[pl.ds(SUB - 3 + j, tb), :] * w_ref[j:j + 1, :]
    y = _silu(conv)
    qscale = jnp.where(sec == 0, G_DK ** -0.5, 1.0)
    for h in range(D_INNER // G_DK):
        yh = y[:, h * G_DK:(h + 1) * G_DK]
        r = lax.rsqrt(jnp.sum(yh * yh, axis=-1, keepdims=True) + EPS) * qscale
        o_ref[:, h * G_DK:(h + 1) * G_DK] = yh * jnp.where(sec == 2, 1.0, r)


def _gdn_conv(z, B, T, conv_w):
    tb = 256
    nt = T // tb
    hb = tb // SUB
    return pl.pallas_call(
        functools.partial(_conv_kernel, tb=tb),
        name="gdn_conv",
        grid=(B, nt, 3),
        in_specs=[pl.BlockSpec((tb, D_INNER), lambda b, i, s: (b * nt + i, s)),
                  pl.BlockSpec((SUB, D_INNER), lambda b, i, s: (jnp.maximum((b * nt + i) * hb - 1, 0), s)),
                  pl.BlockSpec((CONV_W, D_INNER), lambda b, i, s: (0, s))],
        out_specs=pl.BlockSpec((tb, D_INNER), lambda b, i, s: (b * nt + i, s)),
        out_shape=jax.ShapeDtypeStruct((B * T, 3 * D_INNER), F32),
        scratch_shapes=[pltpu.VMEM((tb + SUB, D_INNER), F32)],
        compiler_params=_params("parallel", "parallel", "parallel"),
    )(z, z, conv_w)


def _split_bf(x):
    hi = x.astype(BF)
    return hi, (x - hi.astype(F32)).astype(BF)


def _dot3(a, b):
    ah, al = _split_bf(a)
    bh, bl = _split_bf(b)
    return _dot(ah, bh) + (_dot(ah, bl) + _dot(al, bh))


def _softplus(x):
    return jnp.maximum(x, 0.0) + jnp.log(1.0 + jnp.exp(-jnp.abs(x)))


def _dot3s(lhs, rhs):
    sa = [_split_bf(a) for a in lhs]
    sb = [_split_bf(b) for b in rhs]
    hh = [_dot(a[0], b[0]) for a, b in zip(sa, sb)]
    hl = [_dot(a[0], b[1]) for a, b in zip(sa, sb)]
    lh = [_dot(a[1], b[0]) for a, b in zip(sa, sb)]
    return [x + (y + z) for x, y, z in zip(hh, hl, lh)]


def _unit_lower_inverses(lmats, eye):
    n = len(lmats)
    xs = [-l for l in lmats]
    factors = [[eye + x] for x in xs]
    for _ in range(int(math.log2(lmats[0].shape[0])) - 1):
        xb = [x.astype(BF) for x in xs]
        xs = [_dot(b, b) for b in xb]
        for f, x in zip(factors, xs):
            f.append(eye + x)
    while len(factors[0]) > 1:
        m = len(factors[0]) // 2
        fb = [[f.astype(BF) for f in fs] for fs in factors]
        factors = [[_dot(fb[i][2 * j], fb[i][2 * j + 1]) for j in range(m)] + factors[i][2 * m:] for i in range(n)]
    t0 = [fs[0] for fs in factors]
    res = [eye - d for d in _dot3s([eye + l for l in lmats], t0)]
    return [t + d for t, d in zip(t0, _dot3s(t0, res))]


def _rms_gate(o, gain, gate):
    ms = jnp.mean(o * o, axis=-1, keepdims=True)
    return (o * lax.rsqrt(ms + EPS) * gain * _silu(gate)).astype(BF)


def _gdn_kernel(al_ref, dt_ref, q_ref, k_ref, v_ref, gt_ref, abc_ref, abr_ref, on_ref, o_ref, s_ref,
                n_ref, o0_ref, ap_ref, *, c, nchunk, hb):
    hblk = pl.program_id(1)

    @pl.when(pl.program_id(2) == 0)
    def _():
        s_ref[...] = jnp.zeros_like(s_ref)

    ii = lax.broadcasted_iota(jnp.int32, (c, c), 0)
    jj = lax.broadcasted_iota(jnp.int32, (c, c), 1)
    tri = ii >= jj
    eye = (ii == jj).astype(F32)
    lane = lax.broadcasted_iota(jnp.int32, (c, LANES), 1)
    dk = G_DK
    chains = [(hh, ci) for hh in range(hb) for ci in range(nchunk)]
    st = []
    for hh, ci in chains:
        h = hblk * hb + hh
        neg_a = -jnp.exp(jnp.full((1, 1), 1.0, F32) * al_ref[h])
        dt = dt_ref[h]
        sl = pl.ds(ci * c, c)
        hs = slice(hh * dk, (hh + 1) * dk)
        ab = abc_ref[sl, :]
        a_col = jnp.sum(jnp.where(lane == h, ab, 0.0), axis=-1, keepdims=True)
        b_col = jnp.sum(jnp.where(lane == G_HEADS + h, ab, 0.0), axis=-1, keepdims=True)
        la_col = neg_a * _softplus(a_col + dt)
        la_row = neg_a * _softplus(abr_ref[0, hh, :, sl] + dt)
        g_col = jnp.sum(jnp.where(tri, la_row, 0.0), axis=1, keepdims=True)
        g_row = jnp.sum(jnp.where(ii <= jj, la_col, 0.0), axis=0, keepdims=True)
        gl = jnp.sum(la_row, axis=1, keepdims=True)
        st.append(dict(sl=sl, hs=hs, beta=jax.nn.sigmoid(b_col), g=g_col, gl=gl,
                       dec=jnp.where(tri, jnp.exp(g_col - g_row), 0.0),
                       q=q_ref[sl, hs], k=k_ref[sl, hs], v=v_ref[sl, hs]))
    for s_ in st:
        s_["kb"] = s_["k"].astype(BF)
    kq = [_dot_nt(jnp.concatenate([s_["kb"], s_["q"].astype(BF)], axis=0), s_["kb"]) for s_ in st]
    lmats = [jnp.where(ii > jj, s_["beta"] * x[:c] * s_["dec"], 0.0) for s_, x in zip(st, kq)]
    atts = [(x[c:] * s_["dec"]).astype(BF) for s_, x in zip(st, kq)]
    tinv = [_split_bf(t) for t in _unit_lower_inverses(lmats, eye)]
    rhs = [jnp.concatenate([(s_["v"] * s_["beta"]).astype(BF),
                            (s_["k"] * (s_["beta"] * jnp.exp(s_["g"]))).astype(BF)], axis=1) for s_ in st]
    sol_h = [_dot(t[0], r) for t, r in zip(tinv, rhs)]
    sol_l = [_dot(t[1], r) for t, r in zip(tinv, rhs)]
    uw = [(a + b).astype(BF) for a, b in zip(sol_h, sol_l)]
    kds = [(s_["k"] * jnp.exp(s_["gl"] - s_["g"])).astype(BF) for s_ in st]
    npm = [_dot_tn(kd, x) for kd, x in zip(kds, uw)]
    aow = [_dot(att, x) for att, x in zip(atts, uw)]
    for idx, s_ in enumerate(st):
        n_ref[idx] = npm[idx][:, :G_DV]
        o0_ref[idx] = aow[idx][:, :G_DV]
        ap_ref[idx, 0:c, :] = (s_["q"] * jnp.exp(s_["g"]) - aow[idx][:, G_DV:]).astype(BF)
        ap_ref[idx, c:, :] = npm[idx][:, G_DV:].astype(BF)
    egls = [jnp.exp(s_["gl"]) for s_ in st]
    gain = on_ref[...]
    states = [s_ref[0, hh] for hh in range(hb)]
    for ci in range(nchunk):
        for hh in range(hb):
            idx = hh * nchunk + ci
            s_ = st[idx]
            res = _dot(ap_ref[idx], states[hh].astype(BF))
            o = res[:c] + o0_ref[idx]
            states[hh] = states[hh] * egls[idx] - res[c:] + n_ref[idx]
            o_ref[s_["sl"], s_["hs"]] = _rms_gate(o, gain, gt_ref[s_["sl"], s_["hs"]])
    for hh in range(hb):
        s_ref[0, hh] = states[hh]


def _gdn_prompt(qkv, z, zab, B, T, a_log, dt_bias, onorm):
    c = G_CHUNK
    tb = 512
    hb = 2
    nt = T // tb
    H = G_HEADS
    nh = H // hb
    nchunk = tb // c
    zab_rows = zab[:, :H].reshape(B, T, H).transpose(0, 2, 1)[:, :, None, :]
    return pl.pallas_call(
        functools.partial(_gdn_kernel, c=c, nchunk=nchunk, hb=hb),
        name="gdn_chunks",
        grid=(B, nh, nt),
        in_specs=[pl.BlockSpec(memory_space=pltpu.SMEM),
                  pl.BlockSpec(memory_space=pltpu.SMEM),
                  pl.BlockSpec((tb, hb * G_DK), lambda b, h, i: (b * nt + i, h)),
                  pl.BlockSpec((tb, hb * G_DK), lambda b, h, i: (b * nt + i, nh + h)),
                  pl.BlockSpec((tb, hb * G_DV), lambda b, h, i: (b * nt + i, 2 * nh + h)),
                  pl.BlockSpec((tb, hb * G_DV), lambda b, h, i: (b * nt + i, 3 * nh + h)),
                  pl.BlockSpec((tb, LANES), lambda b, h, i: (b * nt + i, 0)),
                  pl.BlockSpec((1, hb, 1, tb), lambda b, h, i: (b, h, 0, i)),
                  pl.BlockSpec((1, G_DV), lambda b, h, i: (0, 0))],
        out_specs=[pl.BlockSpec((tb, hb * G_DV), lambda b, h, i: (b * nt + i, h)),
                   pl.BlockSpec((1, hb, G_DK, G_DV), lambda b, h, i: (b, h, 0, 0))],
        out_shape=[jax.ShapeDtypeStruct((B * T, D_INNER), BF),
                   jax.ShapeDtypeStruct((B, H, G_DK, G_DV), F32)],
        scratch_shapes=[pltpu.VMEM((hb * nchunk, G_DK, G_DV), F32), pltpu.VMEM((hb * nchunk, c, G_DV), F32),
                        pltpu.VMEM((hb * nchunk, c + G_DK, G_DK), BF)],
        compiler_params=_params("parallel", "parallel", "arbitrary"),
    )(a_log, dt_bias, qkv, qkv, qkv, z, zab, zab_rows, onorm.reshape(1, G_DV))


def _gdn_step_kernel(al_ref, dt_ref, z_ref, ab_ref, c0_ref, w_ref, s0_ref, on_ref, o_ref, s_ref):
    row0 = lax.broadcasted_iota(jnp.int32, (SUB, 1), 0) == 0
    lane = lax.broadcasted_iota(jnp.int32, (1, LANES), 1)
    ab = ab_ref[0]
    c3 = 3 * D_INNER

    def conv_head(col):
        sl = slice(col * G_DK, (col + 1) * G_DK)
        acc = z_ref[0, :, sl] * w_ref[CONV_W - 1:CONV_W, sl]
        for j in range(CONV_W - 1):
            acc = acc + c0_ref[0, j:j + 1, sl] * w_ref[j:j + 1, sl]
        return jnp.where(row0, _silu(acc), 0.0)

    for h in range(G_HEADS):
        neg_a = -jnp.exp(jnp.full((1, 1), 1.0, F32) * al_ref[h])
        a = jnp.sum(jnp.where(lane == h, ab, 0.0), axis=-1, keepdims=True)
        b = jnp.sum(jnp.where(lane == G_HEADS + h, ab, 0.0), axis=-1, keepdims=True)
        eg = jnp.exp(neg_a * _softplus(a + dt_ref[h]))
        beta = jax.nn.sigmoid(b)
        q, k, v = conv_head(h), conv_head(G_HEADS + h), conv_head(2 * G_HEADS + h)
        q = q * lax.rsqrt(jnp.sum(q * q, axis=-1, keepdims=True) + EPS) * (G_DK ** -0.5)
        k = k * lax.rsqrt(jnp.sum(k * k, axis=-1, keepdims=True) + EPS)
        s = s0_ref[0, h]
        sb = s.astype(BF)
        u = v * beta - _dot((k * (beta * eg)).astype(BF), sb)
        att = jnp.sum(q * k, axis=-1, keepdims=True)
        o = _dot((q * eg).astype(BF), sb) + att * u
        s_ref[0, h] = s * eg + _dot_tn(k.astype(BF), u.astype(BF))
        gate = z_ref[0, :, c3 + h * G_DV:c3 + (h + 1) * G_DV]
        o_ref[0, :, h * G_DV:(h + 1) * G_DV] = _rms_gate(o, on_ref[...], gate)


def _gdn_step(z, zab, conv0, conv_w, s0, a_log, dt_bias, onorm):
    DB, N = z.shape
    zp = jnp.pad(z[:, None, :], ((0, 0), (0, SUB - 1), (0, 0)))
    o, s = pl.pallas_call(
        _gdn_step_kernel,
        name="gdn_step",
        grid=(DB,),
        in_specs=[pl.BlockSpec(memory_space=pltpu.SMEM),
                  pl.BlockSpec(memory_space=pltpu.SMEM),
                  pl.BlockSpec((1, SUB, N), lambda b: (b, 0, 0)),
                  pl.BlockSpec((1, 1, LANES), lambda b: (b, 0, 0)),
                  pl.BlockSpec((1, CONV_W - 1, 3 * D_INNER), lambda b: (b, 0, 0)),
                  pl.BlockSpec((CONV_W, 3 * D_INNER), lambda b: (0, 0)),
                  pl.BlockSpec((1, G_HEADS, G_DK, G_DV), lambda b: (b, 0, 0, 0)),
                  pl.BlockSpec((1, G_DV), lambda b: (0, 0))],
        out_specs=[pl.BlockSpec((1, SUB, D_INNER), lambda b: (b, 0, 0)),
                   pl.BlockSpec((1, G_HEADS, G_DK, G_DV), lambda b: (b, 0, 0, 0))],
        out_shape=[jax.ShapeDtypeStruct((DB, SUB, D_INNER), BF),
                   jax.ShapeDtypeStruct(s0.shape, F32)],
        compiler_params=_params("parallel"),
    )(a_log, dt_bias, zp, zab[:, None, :], conv0, conv_w, s0, onorm.reshape(1, G_DV))
    return o[:, 0, :], s


def _pad_cols(w, n):
    return jnp.pad(w, ((0, 0), (0, n - w.shape[1])))


def _gdn_layer(xp, xs, norm, w_in, conv_w, a_log, dt_bias, onorm, w_out, conv0, s0):
    B, T, D = xp.shape
    DB = xs.shape[0]
    c3, c4 = 3 * D_INNER, 4 * D_INNER
    w_main = w_in[:, :c4].astype(BF)
    w_tail = _pad_cols(w_in[:, c4:], LANES).astype(BF)
    w_out_bf = w_out.astype(BF)
    xp2, xs2 = xp.reshape(B * T, D), xs.reshape(DB, D)
    zp, zs = _norm_proj(xp2, norm, w_main, 512), _norm_proj(xs2, norm, w_main, 512)
    abp, abs_ = _norm_proj(xp2, norm, w_tail, LANES), _norm_proj(xs2, norm, w_tail, LANES)
    qkv = _gdn_conv(zp, B, T, conv_w)
    op, sp = _gdn_prompt(qkv, zp, abp, B, T, a_log, dt_bias, onorm)
    os_, ss = _gdn_step(zs, abs_, conv0, conv_w, s0, a_log, dt_bias, onorm)
    conv_p = zp.reshape(B, T, c4)[:, T - (CONV_W - 1):, :c3]
    conv_s = jnp.concatenate([conv0[:, 1:], zs[:, None, :c3]], axis=1)
    xp = _out_proj(op, w_out_bf, xp2).reshape(B, T, D)
    xs = _out_proj(os_, w_out_bf, xs2).reshape(DB, 1, D)
    return xp, xs, conv_p, conv_s, sp, ss


NEG = -1e30
M_INIT = -1e29
BIG = 3e38
GQA = N_HEADS // KV_HEADS
CHUNKS_PER_PAGE = PAGE_SIZE // CMP_STRIDE
PAGES_PER_STEP = 8
KV_ROWS = 2 * KV_HEADS


def _rms(x, gain):
    ms = jnp.mean(x * x, axis=-1, keepdims=True)
    return x * lax.rsqrt(ms + EPS) * gain


def _masked_softmax(s, mask):
    m = jnp.max(jnp.where(mask, s, NEG), axis=-1, keepdims=True)
    e = jnp.where(mask, jnp.exp(s - m), 0.0)
    den = jnp.sum(e, axis=-1, keepdims=True)
    return e / jnp.where(den > 0, den, 1.0)


def _cmp1_kernel(pt_ref, *refs):
    del pt_ref
    pages, w_ref, o_ref = refs[:PAGES_PER_STEP], refs[PAGES_PER_STEP], refs[PAGES_PER_STEP + 1]
    npc = PAGES_PER_STEP * CHUNKS_PER_PAGE
    for c in range(2):
        acc = jnp.zeros((KV_HEADS * npc, 2 * HEAD_DIM), F32)
        for l in range(CMP_STRIDE):
            parts = []
            for g in range(KV_HEADS):
                first = l * KV_ROWS + c * KV_HEADS + g
                for p in range(PAGES_PER_STEP):
                    parts.append(pages[p][pl.ds(first, CHUNKS_PER_PAGE, stride=CMP_STRIDE * KV_ROWS), :])
            lhs = jnp.concatenate(parts, axis=0).astype(BF)
            acc = acc + _dot(lhs, w_ref[c, l])
        for g in range(KV_HEADS):
            o_ref[c, g] = acc[g * npc:(g + 1) * npc]


def _cmp_stage1(rows, page_ids, w1cat):
    n = page_ids.shape[0]
    steps = n // PAGES_PER_STEP
    npc = PAGES_PER_STEP * CHUNKS_PER_PAGE
    page_spec = lambda p: pl.BlockSpec((None, PAGE_SIZE * KV_ROWS, HEAD_DIM),
                                       lambda s, pt: (pt[s * PAGES_PER_STEP + p], 0, 0))
    return pl.pallas_call(
        _cmp1_kernel,
        name="nsa_cmp_mlp1",
        grid_spec=pltpu.PrefetchScalarGridSpec(
            num_scalar_prefetch=1,
            grid=(steps,),
            in_specs=[page_spec(p) for p in range(PAGES_PER_STEP)]
            + [pl.BlockSpec((2, CMP_STRIDE, HEAD_DIM, 2 * HEAD_DIM), lambda s, pt: (0, 0, 0, 0))],
            out_specs=pl.BlockSpec((2, KV_HEADS, npc, 2 * HEAD_DIM), lambda s, pt: (0, 0, s, 0)),
        ),
        out_shape=jax.ShapeDtypeStruct((2, KV_HEADS, n * CHUNKS_PER_PAGE, 2 * HEAD_DIM), F32),
        compiler_params=_params("parallel"),
    )(page_ids, *([rows] * PAGES_PER_STEP), w1cat)


def _cmp2_kernel(ab_ref, pe_ref, w1_ref, w2_ref, qn_ref, ck_ref, cv_ref, *, nc):
    for c in range(2):
        bias = _dot(pe_ref[c], w1_ref[c])[0:1, :]
        ab = ab_ref[c, 0]
        pre = ab[:, :HEAD_DIM] + pltpu.roll(ab[:, HEAD_DIM:], nc - 1, 0) + bias
        out = _dot(_silu(pre).astype(BF), w2_ref[c])
        if c == 0:
            ck_ref[0, 0] = _rms(out, qn_ref[1:2, :]).astype(BF)
        else:
            cv_ref[0, 0] = out.astype(BF)


def _cmp_stage2(ab, nb, pe_flat, w1_flat, w2_bf, qk_norm):
    nc = ab.shape[2] // nb
    shp = jax.ShapeDtypeStruct((nb, KV_HEADS, nc, HEAD_DIM), BF)
    kdim = CMP_LEN * HEAD_DIM
    return pl.pallas_call(
        functools.partial(_cmp2_kernel, nc=nc),
        name="nsa_cmp_mlp2",
        grid=(nb, KV_HEADS),
        in_specs=[pl.BlockSpec((2, 1, nc, 2 * HEAD_DIM), lambda b, g: (0, g, b, 0)),
                  pl.BlockSpec((2, SUB, kdim), lambda b, g: (0, 0, 0)),
                  pl.BlockSpec((2, kdim, HEAD_DIM), lambda b, g: (0, 0, 0)),
                  pl.BlockSpec((2, HEAD_DIM, HEAD_DIM), lambda b, g: (0, 0, 0)),
                  pl.BlockSpec((4, HEAD_DIM), lambda b, g: (0, 0))],
        out_specs=[pl.BlockSpec((1, 1, nc, HEAD_DIM), lambda b, g: (b, g, 0, 0))] * 2,
        out_shape=[shp, shp],
        compiler_params=_params("parallel", "parallel"),
    )(ab, pe_flat, w1_flat, w2_bf, qk_norm)


def _kvprep_kernel(s_ref, w_ref, qn_ref, sk_ref, svt_ref, wk_ref, wvt_ref):
    half = KV_HEADS * HEAD_DIM
    for src, kdst, vdst, row in ((s_ref, sk_ref, svt_ref, 2), (w_ref, wk_ref, wvt_ref, 3)):
        gain = qn_ref[row:row + 1, :]
        for g in range(KV_HEADS):
            sl = slice(g * HEAD_DIM, (g + 1) * HEAD_DIM)
            kdst[:, sl] = _rms(src[:, sl], gain).astype(BF)
            vdst[sl, :] = src[:, half + g * HEAD_DIM:half + (g + 1) * HEAD_DIM].T.astype(BF)


def _kv_prep(z, qk_norm, B, T):
    tb = 512
    nt = T // tb
    half = KV_HEADS * HEAD_DIM
    kshp = jax.ShapeDtypeStruct((B * T, half), BF)
    vshp = jax.ShapeDtypeStruct((B, half, T), BF)
    kspec = pl.BlockSpec((tb, half), lambda b, i: (b * nt + i, 0))
    vspec = pl.BlockSpec((None, half, tb), lambda b, i: (b, 0, i))
    return pl.pallas_call(
        _kvprep_kernel,
        name="nsa_kv_prep",
        grid=(B, nt),
        in_specs=[pl.BlockSpec((tb, NSA_KV), lambda b, i: (b * nt + i, 3)),
                  pl.BlockSpec((tb, NSA_KV), lambda b, i: (b * nt + i, 4)),
                  pl.BlockSpec((4, HEAD_DIM), lambda b, i: (0, 0))],
        out_specs=[kspec, vspec, kspec, vspec],
        out_shape=[kshp, vshp, kshp, vshp],
        compiler_params=_params("parallel", "parallel"),
    )(z, z, qk_norm)


def _block_rank(val, lane, n_blocks):
    rank = jnp.zeros(val.shape, jnp.int32)
    for jp in range(n_blocks):
        col = val[:, jp:jp + 1]
        rank = rank + ((col > val) | ((col == val) & (lane > jp))).astype(jnp.int32)
    return rank


def _gate_col(bg, lane, idx):
    return jnp.sum(jnp.where(lane == idx, bg, 0.0), axis=-1, keepdims=True)


def _nsa_attn_kernel(q_ref, gt_ref, bg_ref, ck_ref, cv_ref, sk_ref, svt_ref, wk_ref, wvt_ref, qn_ref, smap_ref, et_ref,
                     o_ref, acc_ref, m_ref, l_ref, bgt_ref, *, qb, tk, nsub, n_slc):
    g = pl.program_id(1)
    i = pl.program_id(2)
    d = HEAD_DIM
    cols = GQA * qb
    gain = qn_ref[0:1, :] * (d ** -0.5)
    q = jnp.concatenate([_rms(q_ref[:, r * d:(r + 1) * d], gain).astype(BF) for r in range(GQA)], axis=0)
    qpos = i * qb + (lax.broadcasted_iota(jnp.int32, (1, cols), 1) & (qb - 1))

    ncp = ck_ref.shape[2]
    cend = lax.broadcasted_iota(jnp.int32, (ncp, 1), 0) * CMP_STRIDE + (CMP_LEN - 1)
    st = _dot_nt(ck_ref[0, 0], q)
    cmask = cend <= qpos
    cm = jnp.max(jnp.where(cmask, st, NEG), axis=0, keepdims=True)
    ce = jnp.where(cmask, jnp.exp(st - cm), 0.0)
    cden = jnp.sum(ce, axis=0, keepdims=True)
    pc = ce / jnp.where(cden > 0, cden, 1.0)
    oc = _dot_tn(cv_ref[0, 0], pc.astype(BF))
    pcs = pc[:, 0:qb]
    for r in range(1, GQA):
        pcs = pcs + pc[:, r * qb:(r + 1) * qb]
    hi, lo = _split_bf(pcs)
    imp = _dot_tn(smap_ref[...], hi) + _dot_tn(smap_ref[...], lo)
    nsp = -(-n_slc // SUB) * SUB
    blk = lax.broadcasted_iota(jnp.int32, (nsp, qb), 0)
    cur = jnp.right_shift(i * qb + lax.broadcasted_iota(jnp.int32, (1, qb), 1), int(math.log2(SLC_BLOCK)))
    vis = blk <= cur
    forced = vis & ((blk == 0) | (blk >= cur - 1))
    val = jnp.where(forced, BIG, jnp.where(vis, imp[:nsp], -BIG))
    rank = jnp.zeros((nsp, qb), jnp.int32)
    for jp in range(n_slc):
        row = val[jp:jp + 1, :]
        rank = rank + ((row > val) | ((row == val) & (blk > jp))).astype(jnp.int32)
    bias = jnp.where((rank < N_SELECT) & vis, 0.0, NEG)
    if nsp < LANES:
        bias = jnp.concatenate([bias, jnp.zeros((LANES - nsp, qb), F32)], axis=0)
    bias_t = bias.T.astype(BF)
    q_sel = jnp.concatenate([q, jnp.concatenate([bias_t] * GQA, axis=0)], axis=1)

    def flash(score_fn, vt_ref, spans):
        m_ref[...] = jnp.full(m_ref.shape, M_INIT, F32)
        l_ref[...] = jnp.zeros(l_ref.shape, F32)
        acc_ref[...] = jnp.zeros(acc_ref.shape, F32)
        for lo_t, hi_t, mask_fn in spans:
            def body(t, carry, mask_fn=mask_fn):
                starts = [pl.multiple_of(t * (tk * nsub) + j * tk, tk) for j in range(nsub)]
                scores = [score_fn(start) for start in starts]
                for start, s in zip(starts, scores):
                    if mask_fn is not None:
                        s = jnp.where(mask_fn(start + lax.broadcasted_iota(jnp.int32, (tk, 1), 0)), s, NEG)
                    m_prev = m_ref[...]
                    m_new = jnp.maximum(m_prev, jnp.max(s, axis=0, keepdims=True))
                    p = jnp.exp(s - m_new)
                    alpha = jnp.exp(m_prev - m_new)
                    l_ref[...] = alpha * l_ref[...] + jnp.sum(p, axis=0, keepdims=True)
                    acc_ref[...] = alpha * acc_ref[...] + _dot(vt_ref[:, pl.ds(start, tk)], p.astype(BF))
                    m_ref[...] = m_new
                return carry

            lax.fori_loop(lo_t, hi_t, body, 0)
        l = l_ref[...]
        return acc_ref[...] / jnp.where(l > 0, l, 1.0)

    def sel_scores(start):
        keys = jnp.concatenate([sk_ref[pl.ds(start, tk), :], et_ref[pl.ds(start, tk), :]], axis=1)
        return _dot_nt(keys, q_sel)

    shift = int(math.log2(tk * nsub // qb))
    n_full = jnp.right_shift(i, shift)
    osel = flash(sel_scores, svt_ref, [(0, n_full, None), (n_full, n_full + 1, lambda kpos: kpos <= qpos)])
    owin = flash(lambda start: _dot_nt(wk_ref[pl.ds(start, tk), :], q), wvt_ref,
                 [(jnp.right_shift(jnp.maximum(i - WINDOW // qb, 0), shift), n_full + 1,
                   lambda kpos: (kpos <= qpos) & (kpos >= qpos - WINDOW))])

    bgt_ref[...] = jax.nn.sigmoid(bg_ref[...]).T
    for r in range(GQA):
        cs = slice(r * qb, (r + 1) * qb)
        base = g * (GQA * 3) + r * 3
        ot = (bgt_ref[pl.ds(base, 1), :] * oc[:, cs] + bgt_ref[pl.ds(base + 1, 1), :] * osel[:, cs]
              + bgt_ref[pl.ds(base + 2, 1), :] * owin[:, cs])
        o_ref[:, r * d:(r + 1) * d] = (ot.T * _silu(gt_ref[:, r * d:(r + 1) * d])).astype(BF)


def _selection_map(n_cmp, n_slc, rows, cols):
    i = np.arange(n_cmp)[:, None]
    j = np.arange(n_slc)[None, :]
    lo = np.maximum(i * CMP_STRIDE, j * SLC_BLOCK)
    hi = np.minimum(i * CMP_STRIDE + CMP_LEN, (j + 1) * SLC_BLOCK)
    out = np.zeros((rows, cols), np.float32)
    out[:n_cmp, :n_slc] = np.maximum(hi - lo, 0) / CMP_STRIDE
    return jnp.asarray(out, BF)


def _nsa_attn_prompt(z, zbg, ck, cv, sk, svt, wk, wvt, qk_norm, B, T):
    qb = 128
    tk = 256
    nt = T // qb
    n_slc = T // SLC_BLOCK
    nsub = 2
    assert n_slc <= LANES and T % (tk * nsub) == 0
    ncp = ck.shape[2]
    smap = _selection_map(ncp - 1, n_slc, ncp, LANES)
    emat = np.zeros((T, LANES), np.float32)
    emat[np.arange(T), np.arange(T) // SLC_BLOCK] = 1.0
    wq = GQA * HEAD_DIM
    gate0 = (D_INNER + 3 * NSA_KV) // wq
    cols = GQA * qb
    k_spec = pl.BlockSpec((T, HEAD_DIM), lambda b, g, i: (b, g))
    vt_spec = pl.BlockSpec((None, HEAD_DIM, T), lambda b, g, i: (b, g, 0))
    return pl.pallas_call(
        functools.partial(_nsa_attn_kernel, qb=qb, tk=tk, nsub=nsub, n_slc=n_slc),
        name="nsa_attn",
        grid=(B, KV_HEADS, nt),
        in_specs=[pl.BlockSpec((qb, wq), lambda b, g, i: (b * nt + i, g)),
                  pl.BlockSpec((qb, wq), lambda b, g, i: (b * nt + i, gate0 + g)),
                  pl.BlockSpec((qb, LANES), lambda b, g, i: (b * nt + i, 0)),
                  pl.BlockSpec((1, 1, ncp, HEAD_DIM), lambda b, g, i: (b, g, 0, 0)),
                  pl.BlockSpec((1, 1, ncp, HEAD_DIM), lambda b, g, i: (b, g, 0, 0)),
                  k_spec, vt_spec, k_spec, vt_spec,
                  pl.BlockSpec((4, HEAD_DIM), lambda b, g, i: (0, 0)),
                  pl.BlockSpec((ncp, LANES), lambda b, g, i: (0, 0)),
                  pl.BlockSpec((T, LANES), lambda b, g, i: (0, 0))],
        out_specs=pl.BlockSpec((qb, wq), lambda b, g, i: (b * nt + i, g)),
        out_shape=jax.ShapeDtypeStruct((B * T, D_INNER), BF),
        scratch_shapes=[pltpu.VMEM((HEAD_DIM, cols), F32), pltpu.VMEM((1, cols), F32), pltpu.VMEM((1, cols), F32),
                        pltpu.VMEM((LANES, qb), F32)],
        compiler_params=_params("parallel", "parallel", "arbitrary"),
    )(z, z, zbg, ck, cv, sk, svt, wk, wvt, qk_norm, smap, jnp.asarray(emat, BF))


def _dec_cmp_kernel(q_ref, ck_ref, cv_ref, qn_ref, smap_ref, tri_ref, oc_ref, idx_ref, *, n_slc):
    d = HEAD_DIM
    nc = ck_ref.shape[2]
    q = _rms(q_ref[0, 0], qn_ref[0:1, :] * (d ** -0.5)).astype(BF)
    ncol = lax.broadcasted_iota(jnp.int32, (1, nc), 1)
    pc = _masked_softmax(_dot_nt(q, ck_ref[0, 0]), ncol <= nc - 2)
    oc_ref[0, 0] = _dot(pc.astype(BF), cv_ref[0, 0])
    row = lax.broadcasted_iota(jnp.int32, (SUB, 1), 0)
    pcs = jnp.sum(jnp.where(row < GQA, pc, 0.0), axis=0, keepdims=True)
    hi, lo = _split_bf(jnp.broadcast_to(pcs, (SUB, nc)))
    imp = _dot(hi, smap_ref[...]) + _dot(lo, smap_ref[...])
    width = smap_ref.shape[1]
    lane = lax.broadcasted_iota(jnp.int32, (SUB, width), 1)
    cur = n_slc - 1
    vis = lane <= cur
    forced = (lane == 0) | ((lane >= cur - 1) & vis)
    val = jnp.where(forced, BIG, jnp.where(vis, imp, -BIG))
    sel = (_block_rank(val, lane, n_slc) < N_SELECT) & (lane < cur)
    cnt = _dot(sel.astype(BF), tri_ref[...])
    lane_f = lane.astype(F32)
    out_lane = lax.broadcasted_iota(jnp.int32, (SUB, LANES), 1)
    idx = jnp.zeros((SUB, LANES), F32)
    for k in range(N_SELECT - 1):
        pick = sel & (jnp.abs(cnt - (k + 1.0)) < 0.5)
        idx = idx + jnp.where(out_lane == k, jnp.sum(jnp.where(pick, lane_f, 0.0), axis=-1, keepdims=True), 0.0)
    idx_ref[0, 0] = idx.astype(jnp.int32)


def _dec_cmp(q4, ck, cv, qk_norm, n_slc):
    DB, G, nc, d = ck.shape
    width = 2 * LANES
    assert n_slc <= width
    smap = _selection_map(nc - 1, n_slc, nc, width)
    tri = jnp.asarray(np.triu(np.ones((width, width), np.float32)), BF)
    blk = lambda b, g: (b, g, 0, 0)
    return pl.pallas_call(
        functools.partial(_dec_cmp_kernel, n_slc=n_slc),
        name="nsa_dec_cmp",
        grid=(DB, G),
        in_specs=[pl.BlockSpec((1, 1, SUB, d), blk),
                  pl.BlockSpec((1, 1, nc, d), blk),
                  pl.BlockSpec((1, 1, nc, d), blk),
                  pl.BlockSpec((4, d), lambda b, g: (0, 0)),
                  pl.BlockSpec((nc, width), lambda b, g: (0, 0)),
                  pl.BlockSpec((width, width), lambda b, g: (0, 0))],
        out_specs=[pl.BlockSpec((1, 1, SUB, d), blk), pl.BlockSpec((1, 1, SUB, LANES), blk)],
        out_shape=[jax.ShapeDtypeStruct((DB, G, SUB, d), F32), jax.ShapeDtypeStruct((DB, G, SUB, LANES), jnp.int32)],
        compiler_params=_params("parallel", "parallel"),
    )(q4, ck, cv, qk_norm, smap, tri)


def _dec_sel_kernel(tbl_ref, *refs):
    del tbl_ref
    blocks = refs[:KV_HEADS]
    q_ref, z_ref, qn_ref, o_ref, acc_ref, m_ref, l_ref = refs[KV_HEADS:]
    d = HEAD_DIM
    s_id = pl.program_id(1)
    last = pl.num_programs(1) - 1

    @pl.when(s_id == 0)
    def _():
        m_ref[...] = jnp.full(m_ref.shape, NEG, F32)
        l_ref[...] = jnp.zeros(l_ref.shape, F32)
        acc_ref[...] = jnp.zeros(acc_ref.shape, F32)

    qgain = qn_ref[0:1, :] * (d ** -0.5)
    kgain = qn_ref[2:3, :]

    def update(g, s, mask, v):
        m_prev = m_ref[g]
        m_new = jnp.maximum(m_prev, jnp.max(jnp.where(mask, s, NEG), axis=-1, keepdims=True))
        p = jnp.where(mask, jnp.exp(s - m_new), 0.0)
        alpha = jnp.exp(m_prev - m_new)
        l_ref[g] = alpha * l_ref[g] + jnp.sum(p, axis=-1, keepdims=True)
        acc_ref[g] = alpha * acc_ref[g] + _dot(p.astype(BF), v)
        m_ref[g] = m_new

    for g in range(KV_HEADS):
        q = _rms(q_ref[0, g], qgain).astype(BF)
        k = _rms(blocks[g][pl.ds(g, SLC_BLOCK, stride=KV_ROWS), :], kgain).astype(BF)
        v = blocks[g][pl.ds(KV_HEADS + g, SLC_BLOCK, stride=KV_ROWS), :].astype(BF)
        update(g, _dot_nt(q, k), jnp.full((SUB, SLC_BLOCK), True), v)

    @pl.when(s_id == last)
    def _():
        first = lax.broadcasted_iota(jnp.int32, (SUB, SUB), 1) == 0
        half = KV_HEADS * d
        for g in range(KV_HEADS):
            q = _rms(q_ref[0, g], qgain).astype(BF)
            k = _rms(z_ref[0, :, g * d:(g + 1) * d], kgain).astype(BF)
            update(g, _dot_nt(q, k), first, z_ref[0, :, half + g * d:half + (g + 1) * d].astype(BF))
            o_ref[0, g] = acc_ref[g] / l_ref[g]


def _dec_sel(tbl, slc_blocks, q4, zpad, qk_norm, n_blk):
    DB, G, _, d = q4.shape
    kv_specs = [pl.BlockSpec((None, SLC_BLOCK * KV_ROWS, d), lambda b, s, t, g=g: (t[(b * G + g) * n_blk + s], 0, 0))
                for g in range(G)]
    return pl.pallas_call(
        _dec_sel_kernel,
        name="nsa_dec_sel",
        grid_spec=pltpu.PrefetchScalarGridSpec(
            num_scalar_prefetch=1,
            grid=(DB, n_blk),
            in_specs=kv_specs + [pl.BlockSpec((1, G, SUB, d), lambda b, s, t: (b, 0, 0, 0)),
                                 pl.BlockSpec((1, SUB, NSA_KV), lambda b, s, t: (b, 0, 3)),
                                 pl.BlockSpec((4, d), lambda b, s, t: (0, 0))],
            out_specs=pl.BlockSpec((1, G, SUB, d), lambda b, s, t: (b, 0, 0, 0)),
            scratch_shapes=[pltpu.VMEM((G, SUB, d), F32), pltpu.VMEM((G, SUB, 1), F32), pltpu.VMEM((G, SUB, 1), F32)],
        ),
        out_shape=jax.ShapeDtypeStruct((DB, G, SUB, d), F32),
        compiler_params=_params("parallel", "arbitrary"),
    )(tbl, *([slc_blocks] * G), q4, zpad, qk_norm)


def _dec_win_kernel(win_ref, q_ref, z_ref, bg_ref, oc_ref, os_ref, qn_ref, o_ref):
    d = HEAD_DIM
    wlen = win_ref.shape[0] // KV_ROWS
    half = KV_HEADS * d
    kw0 = D_INNER + 2 * NSA_KV
    gate0 = D_INNER + 3 * NSA_KV
    qgain = qn_ref[0:1, :] * (d ** -0.5)
    kgain = qn_ref[3:4, :]
    first = lax.broadcasted_iota(jnp.int32, (SUB, SUB), 1) == 0
    lane = lax.broadcasted_iota(jnp.int32, (1, LANES), 1)
    bg = jax.nn.sigmoid(bg_ref[0])
    o_ref[...] = jnp.zeros(o_ref.shape, BF)
    for g in range(KV_HEADS):
        q = _rms(q_ref[0, g], qgain).astype(BF)
        k = _rms(win_ref[pl.ds(g, wlen, stride=KV_ROWS), :], kgain).astype(BF)
        v = win_ref[pl.ds(KV_HEADS + g, wlen, stride=KV_ROWS), :].astype(BF)
        kn = _rms(z_ref[0, :, kw0 + g * d:kw0 + (g + 1) * d], kgain).astype(BF)
        vn = z_ref[0, :, kw0 + half + g * d:kw0 + half + (g + 1) * d].astype(BF)
        s = _dot_nt(q, k)
        sn = jnp.where(first, _dot_nt(q, kn), NEG)
        m = jnp.maximum(jnp.max(s, axis=-1, keepdims=True), jnp.max(sn, axis=-1, keepdims=True))
        p = jnp.exp(s - m)
        pn = jnp.where(first, jnp.exp(sn - m), 0.0)
        den = jnp.sum(p, axis=-1, keepdims=True) + jnp.sum(pn, axis=-1, keepdims=True)
        ow = (_dot(p.astype(BF), v) + _dot(pn.astype(BF), vn)) / den
        oc, os_ = oc_ref[0, g], os_ref[0, g]
        for r in range(GQA):
            base = g * (GQA * 3) + r * 3
            o = (_gate_col(bg, lane, base) * oc[r:r + 1] + _gate_col(bg, lane, base + 1) * os_[r:r + 1]
                 + _gate_col(bg, lane, base + 2) * ow[r:r + 1])
            col = (g * GQA + r) * d
            o_ref[0, 0:1, col:col + d] = (o * _silu(z_ref[0, 0:1, gate0 + col:gate0 + col + d])).astype(BF)


def _dec_win(win, q4, zpad, zbg, oc, os_, qk_norm):
    DB, G, _, d = q4.shape
    wrows = win.shape[1]
    N = zpad.shape[2]
    blk4 = pl.BlockSpec((1, G, SUB, d), lambda b: (b, 0, 0, 0))
    return pl.pallas_call(
        _dec_win_kernel,
        name="nsa_dec_win",
        grid=(DB,),
        in_specs=[pl.BlockSpec((None, wrows, d), lambda b: (b, 0, 0)),
                  blk4,
                  pl.BlockSpec((1, SUB, N), lambda b: (b, 0, 0)),
                  pl.BlockSpec((1, 1, LANES), lambda b: (b, 0, 0)),
                  blk4, blk4,
                  pl.BlockSpec((4, d), lambda b: (0, 0))],
        out_specs=pl.BlockSpec((1, SUB, D_INNER), lambda b: (b, 0, 0)),
        out_shape=jax.ShapeDtypeStruct((DB, SUB, D_INNER), BF),
        compiler_params=_params("parallel"),
    )(win, q4, zpad, zbg[:, None, :], oc, os_, qk_norm)


def _nsa_layer(xp, xs, norm, w_in, qk_norm, cmp_pe, cmp_w1, cmp_w2, w_out, cache_cmp, cache_slc, win, page_table):
    B, T, D = xp.shape
    DB = xs.shape[0]
    G, d = KV_HEADS, HEAD_DIM
    n_main = 2 * D_INNER + 3 * NSA_KV
    w_main = w_in[:, :n_main].astype(BF)
    w_tail = _pad_cols(w_in[:, n_main:], LANES).astype(BF)
    w_out_bf = w_out.astype(BF)
    xp2, xs2 = xp.reshape(B * T, D), xs.reshape(DB, D)
    zp, zs = _norm_proj(xp2, norm, w_main, 512), _norm_proj(xs2, norm, w_main, 512)
    bgp, bgs = _norm_proj(xp2, norm, w_tail, LANES), _norm_proj(xs2, norm, w_tail, LANES)

    w1cat = jnp.concatenate([cmp_w1[:, :CMP_STRIDE], cmp_w1[:, CMP_STRIDE:]], axis=-1).astype(BF)
    w1_flat = cmp_w1.reshape(2, CMP_LEN * d, d).astype(BF)
    pe_flat = jnp.broadcast_to(cmp_pe.transpose(1, 0, 2).reshape(2, 1, CMP_LEN * d), (2, SUB, CMP_LEN * d)).astype(BF)
    w2_bf = cmp_w2.astype(BF)

    n_pages = (B * T) // PAGE_SIZE
    kvp = zp[:, D_INNER:D_INNER + 3 * NSA_KV].reshape(B, T, 3, 2, G, d)
    win_p = kvp[:, max(0, T - WINDOW):, 2]
    ab = _cmp_stage1(kvp[:, :, 0].reshape(n_pages, PAGE_SIZE * KV_ROWS, d), jnp.arange(n_pages, dtype=jnp.int32), w1cat)
    ckp, cvp = _cmp_stage2(ab, B, pe_flat, w1_flat, w2_bf, qk_norm)
    skp, svtp, wkp, wvtp = _kv_prep(zp, qk_norm, B, T)
    op = _nsa_attn_prompt(zp, bgp, ckp, cvp, skp, svtp, wkp, wvtp, qk_norm, B, T)

    past = page_table.shape[1] * PAGE_SIZE
    n_slc = past // SLC_BLOCK + 1
    n_pool = cache_cmp.shape[0]
    ab = _cmp_stage1(cache_cmp.reshape(n_pool, PAGE_SIZE * KV_ROWS, d), page_table.reshape(-1), w1cat)
    cks, cvs = _cmp_stage2(ab, DB, pe_flat, w1_flat, w2_bf, qk_norm)
    q4 = jnp.pad(zs[:, :D_INNER].reshape(DB, G, GQA, d), ((0, 0), (0, 0), (0, SUB - GQA), (0, 0)))
    zpad = jnp.pad(zs[:, None, :], ((0, 0), (0, SUB - 1), (0, 0)))
    oc, idx = _dec_cmp(q4, cks, cvs, qk_norm, n_slc)
    idx = idx[:, :, 0, :N_SELECT - 1]
    per_page = PAGE_SIZE // SLC_BLOCK
    phys = jnp.take_along_axis(page_table, (idx // per_page).reshape(DB, -1), axis=1).reshape(idx.shape)
    tbl = (phys * per_page + idx % per_page).reshape(-1).astype(jnp.int32)
    os_ = _dec_sel(tbl, cache_slc.reshape(n_pool * per_page, SLC_BLOCK * KV_ROWS, d), q4, zpad, qk_norm, N_SELECT - 1)
    osm = _dec_win(win.reshape(DB, win.shape[1] * KV_ROWS, d), q4, zpad, bgs, oc, os_, qk_norm)[:, 0, :]
    kvs = zs[:, D_INNER:D_INNER + 3 * NSA_KV].reshape(DB, 1, 3, 2, G, d)
    win_s = jnp.concatenate([win[:, 1:], kvs[:, :, 2]], axis=1)

    xp = _out_proj(op, w_out_bf, xp2).reshape(B, T, D)
    xs = _out_proj(osm, w_out_bf, xs2).reshape(DB, 1, D)
    return xp, xs, kvp[:, :, 0], kvs[:, :, 0], kvp[:, :, 1], kvs[:, :, 1], win_p, win_s


def kernel(x_prompt, x_sample, state_l0_ret, cache_l1_cmp, cache_l1_slc, state_l1_win, state_l2_conv, state_l2_gdn, state_l3_ret, page_table, l0_norm, l0_w_in, l0_gn, l0_w_out, l1_norm, l1_w_in, l1_qk_norm, l1_cmp_pe, l1_cmp_w1, l1_cmp_w2, l1_w_out, l2_norm, l2_w_in, l2_conv_w, l2_A_log, l2_dt_bias, l2_onorm, l2_w_out, l3_norm, l3_w_in, l3_gn, l3_w_out):
    xp, xs, r0p, r0s = _retention_layer(x_prompt, x_sample, l0_norm, l0_w_in, l0_gn, l0_w_out, state_l0_ret)
    xp, xs, cp, cs, slp, sls, wp, ws = _nsa_layer(xp, xs, l1_norm, l1_w_in, l1_qk_norm, l1_cmp_pe, l1_cmp_w1, l1_cmp_w2,
                                                  l1_w_out, cache_l1_cmp, cache_l1_slc, state_l1_win, page_table)
    xp, xs, cvp, cvs, gp, gs = _gdn_layer(xp, xs, l2_norm, l2_w_in, l2_conv_w, l2_A_log, l2_dt_bias, l2_onorm, l2_w_out,
                                          state_l2_conv, state_l2_gdn)
    xp, xs, r3p, r3s = _retention_layer(xp, xs, l3_norm, l3_w_in, l3_gn, l3_w_out, state_l3_ret)
    return (xp, xs, r0p, r0s, cp, cs, slp, sls, wp, ws, cvp, cvs, gp, gs, r3p, r3s)
```

```python
import functools
import math

import jax
import jax.numpy as jnp
import numpy as np
from jax import lax
from jax.experimental import pallas as pl
from jax.experimental.pallas import tpu as pltpu

F32 = jnp.float32
BF = jnp.bfloat16
EPS = 1e-6
LANES = 128
VMEM_LIMIT = 56 * 2**20

D_MODEL = 1024
D_INNER = 2 * D_MODEL
PAST_LEN = 8192
PAGE_SIZE = 128
R_HEADS, R_DK, R_DV = 8, 128, 256
ROPE_BASE = 10000.0
N_HEADS, KV_HEADS, HEAD_DIM = 16, 4, 128
CMP_LEN, CMP_STRIDE, SLC_BLOCK, N_SELECT, WINDOW = 32, 16, 64, 16, 512
NSA_KV = 2 * KV_HEADS * HEAD_DIM
G_HEADS, G_DK, G_DV, CONV_W, G_CHUNK = 16, 128, 128, 4, 64


def _params(*sem):
    return pltpu.CompilerParams(dimension_semantics=sem, vmem_limit_bytes=VMEM_LIMIT)


def _dot(a, b):
    return jnp.dot(a, b, preferred_element_type=F32)


def _dot_nt(a, b):
    return lax.dot_general(a, b, (((1,), (1,)), ((), ())), preferred_element_type=F32)


def _dot_tn(a, b):
    return lax.dot_general(a, b, (((0,), (0,)), ((), ())), preferred_element_type=F32)


def _silu(x):
    return x * jax.nn.sigmoid(x)


def _proj_kernel(x_ref, g_ref, w_ref, o_ref, xn_ref):
    @pl.when(pl.program_id(1) == 0)
    def _():
        x = x_ref[...]
        ms = jnp.mean(x * x, axis=-1, keepdims=True)
        xn_ref[...] = (x * lax.rsqrt(ms + EPS) * g_ref[...]).astype(BF)

    o_ref[...] = _dot(xn_ref[...], w_ref[...]).astype(o_ref.dtype)


def _norm_proj(x, gain, w_bf, tn, out_dtype=F32):
    M, D = x.shape
    N = w_bf.shape[1]
    tm = min(M, 1024)
    return pl.pallas_call(
        _proj_kernel,
        name="norm_proj",
        grid=(M // tm, N // tn),
        in_specs=[pl.BlockSpec((tm, D), lambda i, j: (i, 0)),
                  pl.BlockSpec((1, D), lambda i, j: (0, 0)),
                  pl.BlockSpec((D, tn), lambda i, j: (0, j))],
        out_specs=pl.BlockSpec((tm, tn), lambda i, j: (i, j)),
        out_shape=jax.ShapeDtypeStruct((M, N), out_dtype),
        scratch_shapes=[pltpu.VMEM((tm, D), BF)],
        compiler_params=_params("parallel", "arbitrary"),
    )(x, gain.reshape(1, D), w_bf)


KV_ROWS = 8


def _proj_rows_kernel(x_ref, g_ref, w_ref, o_ref, *, tm):
    x = x_ref[...]
    ms = jnp.mean(x * x, axis=-1, keepdims=True)
    xn = (x * lax.rsqrt(ms + EPS) * g_ref[...]).astype(BF)
    for j in range(KV_ROWS // 2):
        res = _dot(xn, w_ref[:, j * 2 * LANES:(j + 1) * 2 * LANES])
        o_ref[pl.ds(2 * j, tm, stride=KV_ROWS), :] = res[:, :LANES]
        o_ref[pl.ds(2 * j + 1, tm, stride=KV_ROWS), :] = res[:, LANES:]


def _norm_proj_rows(x, gain, w_bf):
    M, D = x.shape
    tm = min(M, 512)
    return pl.pallas_call(
        functools.partial(_proj_rows_kernel, tm=tm),
        name="norm_proj_rows",
        grid=(M // tm,),
        in_specs=[pl.BlockSpec((tm, D), lambda i: (i, 0)),
                  pl.BlockSpec((1, D), lambda i: (0, 0)),
                  pl.BlockSpec((D, KV_ROWS * LANES), lambda i: (0, 0))],
        out_specs=pl.BlockSpec((tm * KV_ROWS, LANES), lambda i: (i, 0)),
        out_shape=jax.ShapeDtypeStruct((M * KV_ROWS, LANES), F32),
        compiler_params=_params("parallel"),
    )(x, gain.reshape(1, D), w_bf)


def _out_kernel(a_ref, w_ref, x_ref, o_ref):
    o_ref[...] = x_ref[...] + _dot(a_ref[...], w_ref[...])


def _out_proj(a_bf, w_bf, x):
    M, K = a_bf.shape
    N = w_bf.shape[1]
    tm = min(M, 1024)
    tn = 512
    return pl.pallas_call(
        _out_kernel,
        name="out_proj",
        grid=(M // tm, N // tn),
        in_specs=[pl.BlockSpec((tm, K), lambda i, j: (i, 0)),
                  pl.BlockSpec((K, tn), lambda i, j: (0, j)),
                  pl.BlockSpec((tm, tn), lambda i, j: (i, j))],
        out_specs=pl.BlockSpec((tm, tn), lambda i, j: (i, j)),
        out_shape=jax.ShapeDtypeStruct((M, N), F32),
        compiler_params=_params("parallel", "arbitrary"),
    )(a_bf, w_bf, x)


def _rope_tables(pos):
    half = R_DK // 2
    inv = ROPE_BASE ** (-jnp.arange(half, dtype=F32) / half)
    ang = pos.astype(F32)[:, None] * inv[None, :]
    cos, sin = jnp.cos(ang), jnp.sin(ang)
    return jnp.concatenate([cos, cos], axis=-1), jnp.concatenate([-sin, sin], axis=-1)


def _log_gamma():
    return jnp.log1p(-jnp.exp2(-5.0 - jnp.arange(R_HEADS, dtype=F32)))


def _group_norm_gate(o, gn, gate):
    mu = jnp.mean(o, axis=-1, keepdims=True)
    xc = o - mu
    var = jnp.mean(xc * xc, axis=-1, keepdims=True)
    return (xc * lax.rsqrt(var + EPS) * gn * _silu(gate)).astype(BF)


def _ret_kernel(lg_ref, q_ref, k_ref, v_ref, gt_ref, cos_ref, sin_ref, gn_ref, o_ref, s_ref, *, c, nchunk, hb):
    hblk = pl.program_id(1)

    @pl.when(pl.program_id(2) == 0)
    def _():
        s_ref[...] = jnp.zeros_like(s_ref)

    ii = lax.broadcasted_iota(jnp.int32, (c, c), 0)
    jj = lax.broadcasted_iota(jnp.int32, (c, c), 1)
    ridx = lax.broadcasted_iota(jnp.int32, (c, 1), 0).astype(F32)
    st = []
    for hh in range(hb):
        lg = lg_ref[hblk * hb + hh]
        dec = jnp.where(ii >= jj, jnp.exp((ii - jj).astype(F32) * lg), 0.0)
        eg = jnp.exp((ridx + 1.0) * lg)
        ek = jnp.exp((c - 1.0 - ridx) * lg)
        egl = jnp.exp(jnp.full((1, 1), c, F32) * lg)
        for ci in range(nchunk):
            sl = pl.ds(ci * c, c)
            ks = slice(hh * R_DK, (hh + 1) * R_DK)
            vs = slice(hh * R_DV, (hh + 1) * R_DV)
            cos, sin = cos_ref[sl, :], sin_ref[sl, :]
            q, k = q_ref[sl, ks].astype(F32), k_ref[sl, ks].astype(F32)
            qr = q * cos + pltpu.roll(q, R_DK // 2, 1) * sin
            kr = (k * cos + pltpu.roll(k, R_DK // 2, 1) * sin) * (R_DK ** -0.5)
            st.append(dict(hh=hh, sl=sl, vs=vs, dec=dec, egl=egl, qb=qr.astype(BF), kb=kr.astype(BF),
                           qg=(qr * eg).astype(BF), kd=(kr * ek).astype(BF), v=v_ref[sl, vs].astype(BF)))
    atts = [(_dot_nt(s_["qb"], s_["kb"]) * s_["dec"]).astype(BF) for s_ in st]
    kvs = [_dot_tn(s_["kd"], s_["v"]) for s_ in st]
    intra = [_dot(a, s_["v"]) for a, s_ in zip(atts, st)]
    states = [s_ref[0, hh] for hh in range(hb)]
    inter = []
    for s_, kv in zip(st, kvs):
        inter.append(_dot(s_["qg"], states[s_["hh"]].astype(BF)))
        states[s_["hh"]] = states[s_["hh"]] * s_["egl"] + kv
    for hh in range(hb):
        s_ref[0, hh] = states[hh]
    for s_, a, b in zip(st, intra, inter):
        o_ref[s_["sl"], s_["vs"]] = _group_norm_gate(a + b, gn_ref[:, s_["vs"]], gt_ref[s_["sl"], s_["vs"]].astype(F32))


def _retention_prompt(z, B, T, gn):
    c = 256
    tb = 512
    hb = 2
    nt = T // tb
    nh = R_HEADS // hb
    cos, sin = _rope_tables(jnp.arange(T))
    kv = (2 * R_HEADS * R_DK) // (hb * R_DV)
    return pl.pallas_call(
        functools.partial(_ret_kernel, c=c, nchunk=tb // c, hb=hb),
        name="ret_chunks",
        grid=(B, nh, nt),
        in_specs=[pl.BlockSpec(memory_space=pltpu.SMEM),
                  pl.BlockSpec((tb, hb * R_DK), lambda b, h, i: (b * nt + i, h)),
                  pl.BlockSpec((tb, hb * R_DK), lambda b, h, i: (b * nt + i, nh + h)),
                  pl.BlockSpec((tb, hb * R_DV), lambda b, h, i: (b * nt + i, kv + h)),
                  pl.BlockSpec((tb, hb * R_DV), lambda b, h, i: (b * nt + i, kv + nh + h)),
                  pl.BlockSpec((tb, R_DK), lambda b, h, i: (i, 0)),
                  pl.BlockSpec((tb, R_DK), lambda b, h, i: (i, 0)),
                  pl.BlockSpec((1, hb * R_DV), lambda b, h, i: (0, h))],
        out_specs=[pl.BlockSpec((tb, hb * R_DV), lambda b, h, i: (b * nt + i, h)),
                   pl.BlockSpec((1, hb, R_DK, R_DV), lambda b, h, i: (b, h, 0, 0))],
        out_shape=[jax.ShapeDtypeStruct((B * T, D_INNER), BF),
                   jax.ShapeDtypeStruct((B, R_HEADS, R_DK, R_DV), F32)],
        compiler_params=_params("parallel", "parallel", "arbitrary"),
    )(_log_gamma(), z, z, z, z, cos, sin, gn.reshape(1, D_INNER))


SUB = 8


def _ret_step_kernel(lg_ref, z_ref, s0_ref, cos_ref, sin_ref, gn_ref, o_ref, s_ref):
    cos, sin = cos_ref[...], sin_ref[...]
    qk = R_HEADS * R_DK
    for h in range(R_HEADS):
        gam = jnp.exp(jnp.full((1, 1), 1.0, F32) * lg_ref[h])
        q = z_ref[0, :, h * R_DK:(h + 1) * R_DK]
        k = z_ref[0, :, qk + h * R_DK:qk + (h + 1) * R_DK]
        v = z_ref[0, :, 2 * qk + h * R_DV:2 * qk + (h + 1) * R_DV]
        gate = z_ref[0, :, 2 * qk + D_INNER + h * R_DV:2 * qk + D_INNER + (h + 1) * R_DV]
        qr = q * cos + pltpu.roll(q, R_DK // 2, 1) * sin
        kr = (k * cos + pltpu.roll(k, R_DK // 2, 1) * sin) * (R_DK ** -0.5)
        s = s0_ref[0, h]
        att = jnp.sum(qr * kr, axis=-1, keepdims=True)
        o = att * v + _dot((qr * gam).astype(BF), s.astype(BF))
        s_ref[0, h] = s * gam + _dot_tn(kr.astype(BF), v.astype(BF))
        o_ref[0, :, h * R_DV:(h + 1) * R_DV] = _group_norm_gate(o, gn_ref[:, h * R_DV:(h + 1) * R_DV], gate)


def _retention_step(z, s0, gn):
    DB, N = z.shape
    zp = jnp.pad(z[:, None, :], ((0, 0), (0, SUB - 1), (0, 0)))
    cos, sin = _rope_tables(jnp.full((1,), PAST_LEN))
    o, s = pl.pallas_call(
        _ret_step_kernel,
        name="ret_step",
        grid=(DB,),
        in_specs=[pl.BlockSpec(memory_space=pltpu.SMEM),
                  pl.BlockSpec((1, SUB, N), lambda b: (b, 0, 0)),
                  pl.BlockSpec((1, R_HEADS, R_DK, R_DV), lambda b: (b, 0, 0, 0)),
                  pl.BlockSpec((1, R_DK), lambda b: (0, 0)),
                  pl.BlockSpec((1, R_DK), lambda b: (0, 0)),
                  pl.BlockSpec((1, D_INNER), lambda b: (0, 0))],
        out_specs=[pl.BlockSpec((1, SUB, D_INNER), lambda b: (b, 0, 0)),
                   pl.BlockSpec((1, R_HEADS, R_DK, R_DV), lambda b: (b, 0, 0, 0))],
        out_shape=[jax.ShapeDtypeStruct((DB, SUB, D_INNER), BF),
                   jax.ShapeDtypeStruct(s0.shape, F32)],
        compiler_params=_params("parallel"),
    )(_log_gamma(), zp, s0, cos, sin, gn.reshape(1, D_INNER))
    return o[:, 0, :], s


def _retention_layer(xp, xs, norm, w_in, gn, w_out, s0):
    B, T, D = xp.shape
    DB = xs.shape[0]
    w_in_bf, w_out_bf = w_in.astype(BF), w_out.astype(BF)
    zp = _norm_proj(xp.reshape(B * T, D), norm, w_in_bf, 512, BF)
    zs = _norm_proj(xs.reshape(DB, D), norm, w_in_bf, 512)
    op, sp = _retention_prompt(zp, B, T, gn)
    os_, ss = _retention_step(zs, s0, gn)
    xp = _out_proj(op, w_out_bf, xp.reshape(B * T, D)).reshape(B, T, D)
    xs = _out_proj(os_, w_out_bf, xs.reshape(DB, D)).reshape(DB, 1, D)
    return xp, xs, sp, ss


HALO = 16


def _conv_kernel(x_ref, halo_ref, w_ref, o_ref, ext_ref, *, tb):
    sec = pl.program_id(2)
    first = pl.program_id(1) == 0
    ext_ref[pl.ds(HALO, tb), :] = x_ref[...].astype(F32)
    ext_ref[pl.ds(0, HALO), :] = jnp.where(first, 0.0, halo_ref[...].astype(F32))
    first_tap = HALO - (CONV_W - 1)
    conv = ext_ref[pl.ds(first_tap, tb), :] * w_ref[0:1, :]
    for j in range(1, CONV_W):
        conv = conv + ext_ref[pl.ds(first_tap + j, tb), :] * w_ref[j:j + 1, :]
    y = _silu(conv)
    qscale = jnp.where(sec == 0, G_DK ** -0.5, 1.0)
    for h in range(D_INNER // G_DK):
        yh = y[:, h * G_DK:(h + 1) * G_DK]
        r = lax.rsqrt(jnp.sum(yh * yh, axis=-1, keepdims=True) + EPS) * qscale
        o_ref[:, h * G_DK:(h + 1) * G_DK] = yh * jnp.where(sec == 2, 1.0, r)


def _gdn_conv(z, B, T, conv_w):
    tb = 256
    nt = T // tb
    hb = tb // HALO
    return pl.pallas_call(
        functools.partial(_conv_kernel, tb=tb),
        name="gdn_conv",
        grid=(B, nt, 3),
        in_specs=[pl.BlockSpec((tb, D_INNER), lambda b, i, s: (b * nt + i, s)),
                  pl.BlockSpec((HALO, D_INNER), lambda b, i, s: (jnp.maximum((b * nt + i) * hb - 1, 0), s)),
                  pl.BlockSpec((CONV_W, D_INNER), lambda b, i, s: (0, s))],
        out_specs=pl.BlockSpec((tb, D_INNER), lambda b, i, s: (b * nt + i, s)),
        out_shape=jax.ShapeDtypeStruct((B * T, 3 * D_INNER), F32),
        scratch_shapes=[pltpu.VMEM((tb + HALO, D_INNER), F32)],
        compiler_params=_params("parallel", "parallel", "parallel"),
    )(z, z, conv_w)


def _split_bf(x):
    hi = x.astype(BF)
    return hi, (x - hi.astype(F32)).astype(BF)


def _dot3(a, b):
    ah, al = _split_bf(a)
    bh, bl = _split_bf(b)
    return _dot(ah, bh) + (_dot(ah, bl) + _dot(al, bh))


def _softplus(x):
    return jnp.maximum(x, 0.0) + jnp.log(1.0 + jnp.exp(-jnp.abs(x)))


def _dot3s(lhs, rhs):
    sa = [_split_bf(a) for a in lhs]
    sb = [_split_bf(b) for b in rhs]
    hh = [_dot(a[0], b[0]) for a, b in zip(sa, sb)]
    hl = [_dot(a[0], b[1]) for a, b in zip(sa, sb)]
    lh = [_dot(a[1], b[0]) for a, b in zip(sa, sb)]
    return [x + (y + z) for x, y, z in zip(hh, hl, lh)]


def _unit_lower_inverses(lmats, eye):
    n = len(lmats)
    xs = [-l for l in lmats]
    factors = [[eye + x] for x in xs]
    for _ in range(int(math.log2(lmats[0].shape[0])) - 1):
        xb = [x.astype(BF) for x in xs]
        xs = [_dot(b, b) for b in xb]
        for f, x in zip(factors, xs):
            f.append(eye + x)
    while len(factors[0]) > 1:
        m = len(factors[0]) // 2
        fb = [[f.astype(BF) for f in fs] for fs in factors]
        factors = [[_dot(fb[i][2 * j], fb[i][2 * j + 1]) for j in range(m)] + factors[i][2 * m:] for i in range(n)]
    t0 = [fs[0] for fs in factors]
    res = [eye - d for d in _dot3s([eye + l for l in lmats], t0)]
    return [t + d for t, d in zip(t0, _dot3s(t0, res))]


def _rms_gate(o, gain, gate):
    ms = jnp.mean(o * o, axis=-1, keepdims=True)
    return (o * lax.rsqrt(ms + EPS) * gain * _silu(gate)).astype(BF)


def _gdn_kernel(al_ref, dt_ref, q_ref, k_ref, v_ref, gt_ref, abc_ref, abr_ref, on_ref, o_ref, s_ref,
                n_ref, o0_ref, ap_ref, *, c, nchunk, hb):
    hblk = pl.program_id(1)

    @pl.when(pl.program_id(2) == 0)
    def _():
        s_ref[...] = jnp.zeros_like(s_ref)

    ii = lax.broadcasted_iota(jnp.int32, (c, c), 0)
    jj = lax.broadcasted_iota(jnp.int32, (c, c), 1)
    tri = ii >= jj
    eye = (ii == jj).astype(F32)
    lane = lax.broadcasted_iota(jnp.int32, (c, LANES), 1)
    dk = G_DK
    chains = [(hh, ci) for hh in range(hb) for ci in range(nchunk)]
    st = []
    for hh, ci in chains:
        h = hblk * hb + hh
        neg_a = -jnp.exp(jnp.full((1, 1), 1.0, F32) * al_ref[h])
        dt = dt_ref[h]
        sl = pl.ds(ci * c, c)
        hs = slice(hh * dk, (hh + 1) * dk)
        ab = abc_ref[sl, :]
        a_col = jnp.sum(jnp.where(lane == h, ab, 0.0), axis=-1, keepdims=True)
        b_col = jnp.sum(jnp.where(lane == G_HEADS + h, ab, 0.0), axis=-1, keepdims=True)
        la_col = neg_a * _softplus(a_col + dt)
        la_row = neg_a * _softplus(abr_ref[0, hh, :, sl] + dt)
        g_col = jnp.sum(jnp.where(tri, la_row, 0.0), axis=1, keepdims=True)
        g_row = jnp.sum(jnp.where(ii <= jj, la_col, 0.0), axis=0, keepdims=True)
        gl = jnp.sum(la_row, axis=1, keepdims=True)
        st.append(dict(sl=sl, hs=hs, beta=jax.nn.sigmoid(b_col), g=g_col, gl=gl,
                       dec=jnp.where(tri, jnp.exp(g_col - g_row), 0.0),
                       q=q_ref[sl, hs], k=k_ref[sl, hs], v=v_ref[sl, hs]))
    for s_ in st:
        s_["kb"] = s_["k"].astype(BF)
    kq = [_dot_nt(jnp.concatenate([s_["kb"], s_["q"].astype(BF)], axis=0), s_["kb"]) for s_ in st]
    lmats = [jnp.where(ii > jj, s_["beta"] * x[:c] * s_["dec"], 0.0) for s_, x in zip(st, kq)]
    atts = [(x[c:] * s_["dec"]).astype(BF) for s_, x in zip(st, kq)]
    tinv = [_split_bf(t) for t in _unit_lower_inverses(lmats, eye)]
    rhs = [jnp.concatenate([(s_["v"] * s_["beta"]).astype(BF),
                            (s_["k"] * (s_["beta"] * jnp.exp(s_["g"]))).astype(BF)], axis=1) for s_ in st]
    sol_h = [_dot(t[0], r) for t, r in zip(tinv, rhs)]
    sol_l = [_dot(t[1], r) for t, r in zip(tinv, rhs)]
    uw = [(a + b).astype(BF) for a, b in zip(sol_h, sol_l)]
    kds = [(s_["k"] * jnp.exp(s_["gl"] - s_["g"])).astype(BF) for s_ in st]
    npm = [_dot_tn(kd, x) for kd, x in zip(kds, uw)]
    aow = [_dot(att, x) for att, x in zip(atts, uw)]
    for idx, s_ in enumerate(st):
        n_ref[idx] = npm[idx][:, :G_DV]
        o0_ref[idx] = aow[idx][:, :G_DV]
        ap_ref[idx, 0:c, :] = (s_["q"] * jnp.exp(s_["g"]) - aow[idx][:, G_DV:]).astype(BF)
        ap_ref[idx, c:, :] = npm[idx][:, G_DV:].astype(BF)
    egls = [jnp.exp(s_["gl"]) for s_ in st]
    gain = on_ref[...]
    states = [s_ref[0, hh] for hh in range(hb)]
    for ci in range(nchunk):
        for hh in range(hb):
            idx = hh * nchunk + ci
            s_ = st[idx]
            res = _dot(ap_ref[idx], states[hh].astype(BF))
            o = res[:c] + o0_ref[idx]
            states[hh] = states[hh] * egls[idx] - res[c:] + n_ref[idx]
            o_ref[s_["sl"], s_["hs"]] = _rms_gate(o, gain, gt_ref[s_["sl"], s_["hs"]].astype(F32))
    for hh in range(hb):
        s_ref[0, hh] = states[hh]


def _gdn_prompt(qkv, z, zab, B, T, a_log, dt_bias, onorm):
    c = G_CHUNK
    tb = 512
    hb = 2
    nt = T // tb
    H = G_HEADS
    nh = H // hb
    nchunk = tb // c
    zab_rows = zab[:, :H].reshape(B, T, H).transpose(0, 2, 1)[:, :, None, :]
    return pl.pallas_call(
        functools.partial(_gdn_kernel, c=c, nchunk=nchunk, hb=hb),
        name="gdn_chunks",
        grid=(B, nh, nt),
        in_specs=[pl.BlockSpec(memory_space=pltpu.SMEM),
                  pl.BlockSpec(memory_space=pltpu.SMEM),
                  pl.BlockSpec((tb, hb * G_DK), lambda b, h, i: (b * nt + i, h)),
                  pl.BlockSpec((tb, hb * G_DK), lambda b, h, i: (b * nt + i, nh + h)),
                  pl.BlockSpec((tb, hb * G_DV), lambda b, h, i: (b * nt + i, 2 * nh + h)),
                  pl.BlockSpec((tb, hb * G_DV), lambda b, h, i: (b * nt + i, 3 * nh + h)),
                  pl.BlockSpec((tb, LANES), lambda b, h, i: (b * nt + i, 0)),
                  pl.BlockSpec((1, hb, 1, tb), lambda b, h, i: (b, h, 0, i)),
                  pl.BlockSpec((1, G_DV), lambda b, h, i: (0, 0))],
        out_specs=[pl.BlockSpec((tb, hb * G_DV), lambda b, h, i: (b * nt + i, h)),
                   pl.BlockSpec((1, hb, G_DK, G_DV), lambda b, h, i: (b, h, 0, 0))],
        out_shape=[jax.ShapeDtypeStruct((B * T, D_INNER), BF),
                   jax.ShapeDtypeStruct((B, H, G_DK, G_DV), F32)],
        scratch_shapes=[pltpu.VMEM((hb * nchunk, G_DK, G_DV), F32), pltpu.VMEM((hb * nchunk, c, G_DV), F32),
                        pltpu.VMEM((hb * nchunk, c + G_DK, G_DK), BF)],
        compiler_params=_params("parallel", "parallel", "arbitrary"),
    )(a_log, dt_bias, qkv, qkv, qkv, z, zab, zab_rows, onorm.reshape(1, G_DV))


def _gdn_step_kernel(al_ref, dt_ref, z_ref, ab_ref, c0_ref, w_ref, s0_ref, on_ref, o_ref, s_ref):
    row0 = lax.broadcasted_iota(jnp.int32, (SUB, 1), 0) == 0
    lane = lax.broadcasted_iota(jnp.int32, (1, LANES), 1)
    ab = ab_ref[0]
    c3 = 3 * D_INNER

    def conv_head(col):
        sl = slice(col * G_DK, (col + 1) * G_DK)
        acc = z_ref[0, :, sl] * w_ref[CONV_W - 1:CONV_W, sl]
        for j in range(CONV_W - 1):
            acc = acc + c0_ref[0, j:j + 1, sl] * w_ref[j:j + 1, sl]
        return jnp.where(row0, _silu(acc), 0.0)

    for h in range(G_HEADS):
        neg_a = -jnp.exp(jnp.full((1, 1), 1.0, F32) * al_ref[h])
        a = jnp.sum(jnp.where(lane == h, ab, 0.0), axis=-1, keepdims=True)
        b = jnp.sum(jnp.where(lane == G_HEADS + h, ab, 0.0), axis=-1, keepdims=True)
        eg = jnp.exp(neg_a * _softplus(a + dt_ref[h]))
        beta = jax.nn.sigmoid(b)
        q, k, v = conv_head(h), conv_head(G_HEADS + h), conv_head(2 * G_HEADS + h)
        q = q * lax.rsqrt(jnp.sum(q * q, axis=-1, keepdims=True) + EPS) * (G_DK ** -0.5)
        k = k * lax.rsqrt(jnp.sum(k * k, axis=-1, keepdims=True) + EPS)
        s = s0_ref[0, h]
        sb = s.astype(BF)
        u = v * beta - _dot((k * (beta * eg)).astype(BF), sb)
        att = jnp.sum(q * k, axis=-1, keepdims=True)
        o = _dot((q * eg).astype(BF), sb) + att * u
        s_ref[0, h] = s * eg + _dot_tn(k.astype(BF), u.astype(BF))
        gate = z_ref[0, :, c3 + h * G_DV:c3 + (h + 1) * G_DV]
        o_ref[0, :, h * G_DV:(h + 1) * G_DV] = _rms_gate(o, on_ref[...], gate)


def _gdn_step(z, zab, conv0, conv_w, s0, a_log, dt_bias, onorm):
    DB, N = z.shape
    zp = jnp.pad(z[:, None, :], ((0, 0), (0, SUB - 1), (0, 0)))
    o, s = pl.pallas_call(
        _gdn_step_kernel,
        name="gdn_step",
        grid=(DB,),
        in_specs=[pl.BlockSpec(memory_space=pltpu.SMEM),
                  pl.BlockSpec(memory_space=pltpu.SMEM),
                  pl.BlockSpec((1, SUB, N), lambda b: (b, 0, 0)),
                  pl.BlockSpec((1, 1, LANES), lambda b: (b, 0, 0)),
                  pl.BlockSpec((1, CONV_W - 1, 3 * D_INNER), lambda b: (b, 0, 0)),
                  pl.BlockSpec((CONV_W, 3 * D_INNER), lambda b: (0, 0)),
                  pl.BlockSpec((1, G_HEADS, G_DK, G_DV), lambda b: (b, 0, 0, 0)),
                  pl.BlockSpec((1, G_DV), lambda b: (0, 0))],
        out_specs=[pl.BlockSpec((1, SUB, D_INNER), lambda b: (b, 0, 0)),
                   pl.BlockSpec((1, G_HEADS, G_DK, G_DV), lambda b: (b, 0, 0, 0))],
        out_shape=[jax.ShapeDtypeStruct((DB, SUB, D_INNER), BF),
                   jax.ShapeDtypeStruct(s0.shape, F32)],
        compiler_params=_params("parallel"),
    )(a_log, dt_bias, zp, zab[:, None, :], conv0, conv_w, s0, onorm.reshape(1, G_DV))
    return o[:, 0, :], s


def _pad_cols(w, n):
    return jnp.pad(w, ((0, 0), (0, n - w.shape[1])))


def _gdn_layer(xp, xs, norm, w_in, conv_w, a_log, dt_bias, onorm, w_out, conv0, s0):
    B, T, D = xp.shape
    DB = xs.shape[0]
    c3, c4 = 3 * D_INNER, 4 * D_INNER
    w_main = w_in[:, :c4].astype(BF)
    w_tail = _pad_cols(w_in[:, c4:], LANES).astype(BF)
    w_out_bf = w_out.astype(BF)
    xp2, xs2 = xp.reshape(B * T, D), xs.reshape(DB, D)
    zp, zs = _norm_proj(xp2, norm, w_main, 512, BF), _norm_proj(xs2, norm, w_main, 512)
    abp, abs_ = _norm_proj(xp2, norm, w_tail, LANES), _norm_proj(xs2, norm, w_tail, LANES)
    qkv = _gdn_conv(zp, B, T, conv_w)
    op, sp = _gdn_prompt(qkv, zp, abp, B, T, a_log, dt_bias, onorm)
    os_, ss = _gdn_step(zs, abs_, conv0, conv_w, s0, a_log, dt_bias, onorm)
    n_keep = CONV_W - 1
    x_last = jnp.pad(xp[:, T - n_keep:].reshape(B * n_keep, D), ((0, -(B * n_keep) % HALO), (0, 0)))
    conv_p = _norm_proj(x_last, norm, w_main[:, :c3], 512)[:B * n_keep].reshape(B, n_keep, c3)
    conv_s = jnp.concatenate([conv0[:, 1:], zs[:, None, :c3]], axis=1)
    xp = _out_proj(op, w_out_bf, xp2).reshape(B, T, D)
    xs = _out_proj(os_, w_out_bf, xs2).reshape(DB, 1, D)
    return xp, xs, conv_p, conv_s, sp, ss


NEG = -1e30
M_INIT = -1e29
BIG = 3e38
GQA = N_HEADS // KV_HEADS
CHUNKS_PER_PAGE = PAGE_SIZE // CMP_STRIDE
PAGES_PER_STEP = 8
assert KV_ROWS == 2 * KV_HEADS


def _rms(x, gain):
    ms = jnp.mean(x * x, axis=-1, keepdims=True)
    return x * lax.rsqrt(ms + EPS) * gain


def _masked_softmax(s, mask):
    m = jnp.max(jnp.where(mask, s, NEG), axis=-1, keepdims=True)
    e = jnp.where(mask, jnp.exp(s - m), 0.0)
    den = jnp.sum(e, axis=-1, keepdims=True)
    return e / jnp.where(den > 0, den, 1.0)


def _cmp1_kernel(pt_ref, *refs):
    del pt_ref
    pages, w_ref, o_ref = refs[:PAGES_PER_STEP], refs[PAGES_PER_STEP], refs[PAGES_PER_STEP + 1]
    npc = PAGES_PER_STEP * CHUNKS_PER_PAGE
    for c in range(2):
        cols = []
        for l in range(CMP_STRIDE):
            parts = []
            for g in range(KV_HEADS):
                first = l * KV_ROWS + c * KV_HEADS + g
                for p in range(PAGES_PER_STEP):
                    parts.append(pages[p][pl.ds(first, CHUNKS_PER_PAGE, stride=CMP_STRIDE * KV_ROWS), :])
            cols.append(jnp.concatenate(parts, axis=0).astype(BF))
        acc = _dot(jnp.concatenate(cols, axis=1), w_ref[c])
        for g in range(KV_HEADS):
            o_ref[c, g] = acc[g * npc:(g + 1) * npc]


def _cmp_stage1(rows, page_ids, w1cat):
    n = page_ids.shape[0]
    steps = n // PAGES_PER_STEP
    npc = PAGES_PER_STEP * CHUNKS_PER_PAGE
    page_spec = lambda p: pl.BlockSpec((None, PAGE_SIZE * KV_ROWS, HEAD_DIM),
                                       lambda s, pt: (pt[s * PAGES_PER_STEP + p], 0, 0))
    return pl.pallas_call(
        _cmp1_kernel,
        name="nsa_cmp_mlp1",
        grid_spec=pltpu.PrefetchScalarGridSpec(
            num_scalar_prefetch=1,
            grid=(steps,),
            in_specs=[page_spec(p) for p in range(PAGES_PER_STEP)]
            + [pl.BlockSpec((2, CMP_STRIDE * HEAD_DIM, 2 * HEAD_DIM), lambda s, pt: (0, 0, 0))],
            out_specs=pl.BlockSpec((2, KV_HEADS, npc, 2 * HEAD_DIM), lambda s, pt: (0, 0, s, 0)),
        ),
        out_shape=jax.ShapeDtypeStruct((2, KV_HEADS, n * CHUNKS_PER_PAGE, 2 * HEAD_DIM), F32),
        compiler_params=_params("parallel"),
    )(page_ids, *([rows] * PAGES_PER_STEP), w1cat)


def _cmp2_kernel(ab_ref, pe_ref, w1_ref, w2_ref, qn_ref, ck_ref, cv_ref, *, nc):
    for c in range(2):
        bias = _dot(pe_ref[c], w1_ref[c])[0:1, :]
        ab = ab_ref[c, 0]
        pre = ab[:, :HEAD_DIM] + pltpu.roll(ab[:, HEAD_DIM:], nc - 1, 0) + bias
        out = _dot(_silu(pre).astype(BF), w2_ref[c])
        if c == 0:
            ck_ref[0, 0] = _rms(out, qn_ref[1:2, :]).astype(BF)
        else:
            cv_ref[0, 0] = out.astype(BF)


def _cmp_stage2(ab, nb, pe_flat, w1_flat, w2_bf, qk_norm):
    nc = ab.shape[2] // nb
    shp = jax.ShapeDtypeStruct((nb, KV_HEADS, nc, HEAD_DIM), BF)
    kdim = CMP_LEN * HEAD_DIM
    return pl.pallas_call(
        functools.partial(_cmp2_kernel, nc=nc),
        name="nsa_cmp_mlp2",
        grid=(nb, KV_HEADS),
        in_specs=[pl.BlockSpec((2, 1, nc, 2 * HEAD_DIM), lambda b, g: (0, g, b, 0)),
                  pl.BlockSpec((2, SUB, kdim), lambda b, g: (0, 0, 0)),
                  pl.BlockSpec((2, kdim, HEAD_DIM), lambda b, g: (0, 0, 0)),
                  pl.BlockSpec((2, HEAD_DIM, HEAD_DIM), lambda b, g: (0, 0, 0)),
                  pl.BlockSpec((4, HEAD_DIM), lambda b, g: (0, 0))],
        out_specs=[pl.BlockSpec((1, 1, nc, HEAD_DIM), lambda b, g: (b, g, 0, 0))] * 2,
        out_shape=[shp, shp],
        compiler_params=_params("parallel", "parallel"),
    )(ab, pe_flat, w1_flat, w2_bf, qk_norm)


def _kvprep_kernel(s_ref, w_ref, qn_ref, sk_ref, svt_ref, wk_ref, wvt_ref, *, tb):
    for src, kdst, vdst, row in ((s_ref, sk_ref, svt_ref, 2), (w_ref, wk_ref, wvt_ref, 3)):
        gain = qn_ref[row:row + 1, :]
        for g in range(KV_HEADS):
            sl = slice(g * HEAD_DIM, (g + 1) * HEAD_DIM)
            kdst[:, sl] = _rms(src[pl.ds(g, tb, stride=KV_ROWS), :], gain).astype(BF)
            vdst[sl, :] = src[pl.ds(KV_HEADS + g, tb, stride=KV_ROWS), :].T.astype(BF)


def _kv_prep(kv_s, kv_w, qk_norm, B, T):
    tb = 512
    nt = T // tb
    half = KV_HEADS * HEAD_DIM
    kshp = jax.ShapeDtypeStruct((B * T, half), BF)
    vshp = jax.ShapeDtypeStruct((B, half, T), BF)
    kspec = pl.BlockSpec((tb, half), lambda b, i: (b * nt + i, 0))
    vspec = pl.BlockSpec((None, half, tb), lambda b, i: (b, 0, i))
    rows_spec = pl.BlockSpec((tb * KV_ROWS, HEAD_DIM), lambda b, i: (b * nt + i, 0))
    return pl.pallas_call(
        functools.partial(_kvprep_kernel, tb=tb),
        name="nsa_kv_prep",
        grid=(B, nt),
        in_specs=[rows_spec, rows_spec, pl.BlockSpec((4, HEAD_DIM), lambda b, i: (0, 0))],
        out_specs=[kspec, vspec, kspec, vspec],
        out_shape=[kshp, vshp, kshp, vshp],
        compiler_params=_params("parallel", "parallel"),
    )(kv_s, kv_w, qk_norm)


def _block_rank(val, lane, n_blocks):
    rank = jnp.zeros(val.shape, jnp.int32)
    for jp in range(n_blocks):
        col = val[:, jp:jp + 1]
        rank = rank + ((col > val) | ((col == val) & (lane > jp))).astype(jnp.int32)
    return rank


def _gate_col(bg, lane, idx):
    return jnp.sum(jnp.where(lane == idx, bg, 0.0), axis=-1, keepdims=True)


def _nsa_attn_kernel(q_ref, gt_ref, bg_ref, ck_ref, cv_ref, sk_ref, svt_ref, wk_ref, wvt_ref, qn_ref, smap_ref, et_ref,
                     o_ref, acc_ref, m_ref, l_ref, bgt_ref, *, qb, tk, nsub, n_slc):
    g = pl.program_id(1)
    i = pl.program_id(2)
    d = HEAD_DIM
    cols = GQA * qb
    gain = qn_ref[0:1, :] * (d ** -0.5)
    q = jnp.concatenate([_rms(q_ref[:, r * d:(r + 1) * d].astype(F32), gain).astype(BF) for r in range(GQA)], axis=0)
    qpos = i * qb + (lax.broadcasted_iota(jnp.int32, (1, cols), 1) & (qb - 1))

    ncp = ck_ref.shape[2]
    cend = lax.broadcasted_iota(jnp.int32, (ncp, 1), 0) * CMP_STRIDE + (CMP_LEN - 1)
    st = _dot_nt(ck_ref[0, 0], q)
    cmask = cend <= qpos
    cm = jnp.max(jnp.where(cmask, st, NEG), axis=0, keepdims=True)
    ce = jnp.where(cmask, jnp.exp(st - cm), 0.0)
    cden = jnp.sum(ce, axis=0, keepdims=True)
    pc = ce / jnp.where(cden > 0, cden, 1.0)
    oc = _dot_tn(cv_ref[0, 0], pc.astype(BF))
    pcs = pc[:, 0:qb]
    for r in range(1, GQA):
        pcs = pcs + pc[:, r * qb:(r + 1) * qb]
    hi, lo = _split_bf(pcs)
    imp = _dot_tn(smap_ref[...], hi) + _dot_tn(smap_ref[...], lo)
    nsp = -(-n_slc // SUB) * SUB
    blk = lax.broadcasted_iota(jnp.int32, (nsp, qb), 0)
    cur = jnp.right_shift(i * qb + lax.broadcasted_iota(jnp.int32, (1, qb), 1), int(math.log2(SLC_BLOCK)))
    vis = blk <= cur
    forced = vis & ((blk == 0) | (blk >= cur - 1))
    val = jnp.where(forced, BIG, jnp.where(vis, imp[:nsp], -BIG))
    rank = jnp.zeros((nsp, qb), jnp.int32)
    for jp in range(n_slc):
        row = val[jp:jp + 1, :]
        rank = rank + ((row > val) | ((row == val) & (blk > jp))).astype(jnp.int32)
    bias = jnp.where((rank < N_SELECT) & vis, 0.0, NEG)
    if nsp < LANES:
        bias = jnp.concatenate([bias, jnp.zeros((LANES - nsp, qb), F32)], axis=0)
    bias_t = bias.T.astype(BF)
    q_sel = jnp.concatenate([q, jnp.concatenate([bias_t] * GQA, axis=0)], axis=1)

    def flash(score_fn, vt_ref, spans):
        m_ref[...] = jnp.full(m_ref.shape, M_INIT, F32)
        l_ref[...] = jnp.zeros(l_ref.shape, F32)
        acc_ref[...] = jnp.zeros(acc_ref.shape, F32)
        for lo_t, hi_t, mask_fn in spans:
            def body(t, carry, mask_fn=mask_fn):
                starts = [pl.multiple_of(t * (tk * nsub) + j * tk, tk) for j in range(nsub)]
                scores = [score_fn(start) for start in starts]
                for start, s in zip(starts, scores):
                    if mask_fn is not None:
                        s = jnp.where(mask_fn(start + lax.broadcasted_iota(jnp.int32, (tk, 1), 0)), s, NEG)
                    m_prev = m_ref[...]
                    m_new = jnp.maximum(m_prev, jnp.max(s, axis=0, keepdims=True))
                    p = jnp.exp(s - m_new)
                    alpha = jnp.exp(m_prev - m_new)
                    l_ref[...] = alpha * l_ref[...] + jnp.sum(p, axis=0, keepdims=True)
                    acc_ref[...] = alpha * acc_ref[...] + _dot(vt_ref[:, pl.ds(start, tk)], p.astype(BF))
                    m_ref[...] = m_new
                return carry

            lax.fori_loop(lo_t, hi_t, body, 0)
        l = l_ref[...]
        return acc_ref[...] / jnp.where(l > 0, l, 1.0)

    def sel_scores(start):
        keys = jnp.concatenate([sk_ref[pl.ds(start, tk), :], et_ref[pl.ds(start, tk), :]], axis=1)
        return _dot_nt(keys, q_sel)

    shift = int(math.log2(tk * nsub // qb))
    n_full = jnp.right_shift(i, shift)
    osel = flash(sel_scores, svt_ref, [(0, n_full, None), (n_full, n_full + 1, lambda kpos: kpos <= qpos)])
    owin = flash(lambda start: _dot_nt(wk_ref[pl.ds(start, tk), :], q), wvt_ref,
                 [(jnp.right_shift(jnp.maximum(i - WINDOW // qb, 0), shift), n_full + 1,
                   lambda kpos: (kpos <= qpos) & (kpos >= qpos - WINDOW))])

    bgt_ref[...] = jax.nn.sigmoid(bg_ref[...]).T
    for r in range(GQA):
        cs = slice(r * qb, (r + 1) * qb)
        base = g * (GQA * 3) + r * 3
        ot = (bgt_ref[pl.ds(base, 1), :] * oc[:, cs] + bgt_ref[pl.ds(base + 1, 1), :] * osel[:, cs]
              + bgt_ref[pl.ds(base + 2, 1), :] * owin[:, cs])
        o_ref[:, r * d:(r + 1) * d] = (ot.T * _silu(gt_ref[:, r * d:(r + 1) * d].astype(F32))).astype(BF)


def _selection_map(n_cmp, n_slc, rows, cols):
    i = np.arange(n_cmp)[:, None]
    j = np.arange(n_slc)[None, :]
    lo = np.maximum(i * CMP_STRIDE, j * SLC_BLOCK)
    hi = np.minimum(i * CMP_STRIDE + CMP_LEN, (j + 1) * SLC_BLOCK)
    out = np.zeros((rows, cols), np.float32)
    out[:n_cmp, :n_slc] = np.maximum(hi - lo, 0) / CMP_STRIDE
    return jnp.asarray(out, BF)


def _nsa_attn_prompt(z, zbg, ck, cv, sk, svt, wk, wvt, qk_norm, B, T):
    qb = 128
    tk = 256
    nt = T // qb
    n_slc = T // SLC_BLOCK
    nsub = 2
    assert n_slc <= LANES and T % (tk * nsub) == 0
    ncp = ck.shape[2]
    smap = _selection_map(ncp - 1, n_slc, ncp, LANES)
    emat = np.zeros((T, LANES), np.float32)
    emat[np.arange(T), np.arange(T) // SLC_BLOCK] = 1.0
    wq = GQA * HEAD_DIM
    gate0 = D_INNER // wq
    cols = GQA * qb
    k_spec = pl.BlockSpec((T, HEAD_DIM), lambda b, g, i: (b, g))
    vt_spec = pl.BlockSpec((None, HEAD_DIM, T), lambda b, g, i: (b, g, 0))
    return pl.pallas_call(
        functools.partial(_nsa_attn_kernel, qb=qb, tk=tk, nsub=nsub, n_slc=n_slc),
        name="nsa_attn",
        grid=(B, KV_HEADS, nt),
        in_specs=[pl.BlockSpec((qb, wq), lambda b, g, i: (b * nt + i, g)),
                  pl.BlockSpec((qb, wq), lambda b, g, i: (b * nt + i, gate0 + g)),
                  pl.BlockSpec((qb, LANES), lambda b, g, i: (b * nt + i, 0)),
                  pl.BlockSpec((1, 1, ncp, HEAD_DIM), lambda b, g, i: (b, g, 0, 0)),
                  pl.BlockSpec((1, 1, ncp, HEAD_DIM), lambda b, g, i: (b, g, 0, 0)),
                  k_spec, vt_spec, k_spec, vt_spec,
                  pl.BlockSpec((4, HEAD_DIM), lambda b, g, i: (0, 0)),
                  pl.BlockSpec((ncp, LANES), lambda b, g, i: (0, 0)),
                  pl.BlockSpec((T, LANES), lambda b, g, i: (0, 0))],
        out_specs=pl.BlockSpec((qb, wq), lambda b, g, i: (b * nt + i, g)),
        out_shape=jax.ShapeDtypeStruct((B * T, D_INNER), BF),
        scratch_shapes=[pltpu.VMEM((HEAD_DIM, cols), F32), pltpu.VMEM((1, cols), F32), pltpu.VMEM((1, cols), F32),
                        pltpu.VMEM((LANES, qb), F32)],
        compiler_params=_params("parallel", "parallel", "arbitrary"),
    )(z, z, zbg, ck, cv, sk, svt, wk, wvt, qk_norm, smap, jnp.asarray(emat, BF))


def _dec_cmp_kernel(q_ref, ck_ref, cv_ref, qn_ref, smap_ref, tri_ref, oc_ref, idx_ref, *, n_slc):
    d = HEAD_DIM
    nc = ck_ref.shape[2]
    q = _rms(q_ref[0, 0], qn_ref[0:1, :] * (d ** -0.5)).astype(BF)
    ncol = lax.broadcasted_iota(jnp.int32, (1, nc), 1)
    pc = _masked_softmax(_dot_nt(q, ck_ref[0, 0]), ncol <= nc - 2)
    oc_ref[0, 0] = _dot(pc.astype(BF), cv_ref[0, 0])
    row = lax.broadcasted_iota(jnp.int32, (SUB, 1), 0)
    pcs = jnp.sum(jnp.where(row < GQA, pc, 0.0), axis=0, keepdims=True)
    hi, lo = _split_bf(jnp.broadcast_to(pcs, (SUB, nc)))
    imp = _dot(hi, smap_ref[...]) + _dot(lo, smap_ref[...])
    width = smap_ref.shape[1]
    lane = lax.broadcasted_iota(jnp.int32, (SUB, width), 1)
    cur = n_slc - 1
    vis = lane <= cur
    forced = (lane == 0) | ((lane >= cur - 1) & vis)
    val = jnp.where(forced, BIG, jnp.where(vis, imp, -BIG))
    sel = (_block_rank(val, lane, n_slc) < N_SELECT) & (lane < cur)
    cnt = _dot(sel.astype(BF), tri_ref[...])
    lane_f = lane.astype(F32)
    out_lane = lax.broadcasted_iota(jnp.int32, (SUB, LANES), 1)
    idx = jnp.zeros((SUB, LANES), F32)
    for k in range(N_SELECT - 1):
        pick = sel & (jnp.abs(cnt - (k + 1.0)) < 0.5)
        idx = idx + jnp.where(out_lane == k, jnp.sum(jnp.where(pick, lane_f, 0.0), axis=-1, keepdims=True), 0.0)
    idx_ref[0, 0] = idx.astype(jnp.int32)


def _dec_cmp(q4, ck, cv, qk_norm, n_slc):
    DB, G, nc, d = ck.shape
    width = 2 * LANES
    assert n_slc <= width
    smap = _selection_map(nc - 1, n_slc, nc, width)
    tri = jnp.asarray(np.triu(np.ones((width, width), np.float32)), BF)
    blk = lambda b, g: (b, g, 0, 0)
    return pl.pallas_call(
        functools.partial(_dec_cmp_kernel, n_slc=n_slc),
        name="nsa_dec_cmp",
        grid=(DB, G),
        in_specs=[pl.BlockSpec((1, 1, SUB, d), blk),
                  pl.BlockSpec((1, 1, nc, d), blk),
                  pl.BlockSpec((1, 1, nc, d), blk),
                  pl.BlockSpec((4, d), lambda b, g: (0, 0)),
                  pl.BlockSpec((nc, width), lambda b, g: (0, 0)),
                  pl.BlockSpec((width, width), lambda b, g: (0, 0))],
        out_specs=[pl.BlockSpec((1, 1, SUB, d), blk), pl.BlockSpec((1, 1, SUB, LANES), blk)],
        out_shape=[jax.ShapeDtypeStruct((DB, G, SUB, d), F32), jax.ShapeDtypeStruct((DB, G, SUB, LANES), jnp.int32)],
        compiler_params=_params("parallel", "parallel"),
    )(q4, ck, cv, qk_norm, smap, tri)


def _dec_sel_kernel(tbl_ref, *refs):
    del tbl_ref
    n_blk = (len(refs) - 4) // KV_HEADS
    blocks = refs[:KV_HEADS * n_blk]
    q_ref, new_ref, qn_ref, o_ref = refs[KV_HEADS * n_blk:]
    d = HEAD_DIM
    qgain = qn_ref[0:1, :] * (d ** -0.5)
    kgain = qn_ref[2:3, :]
    for g in range(KV_HEADS):
        qf = _rms(q_ref[0, g], qgain)
        q = qf.astype(BF)
        mine = blocks[g * n_blk:(g + 1) * n_blk]
        ks = [_rms(blk[pl.ds(g, SLC_BLOCK, stride=KV_ROWS), :], kgain).astype(BF) for blk in mine]
        ss = [_dot_nt(q, k) for k in ks]
        s_new = jnp.sum(qf * _rms(new_ref[g:g + 1, :], kgain), axis=-1, keepdims=True)
        m = s_new
        for s in ss:
            m = jnp.maximum(m, jnp.max(s, axis=-1, keepdims=True))
        ps = [jnp.exp(s - m) for s in ss]
        p_new = jnp.exp(s_new - m)
        den = p_new
        for p in ps:
            den = den + jnp.sum(p, axis=-1, keepdims=True)
        pv = [_dot(p.astype(BF), blk[pl.ds(KV_HEADS + g, SLC_BLOCK, stride=KV_ROWS), :].astype(BF))
              for p, blk in zip(ps, mine)]
        acc = p_new * new_ref[KV_HEADS + g:KV_HEADS + g + 1, :]
        for x in pv:
            acc = acc + x
        o_ref[0, g] = acc / den


def _dec_sel(tbl, slc_blocks, q4, kv_new, qk_norm, n_blk):
    DB, G, _, d = q4.shape
    kv_specs = [pl.BlockSpec((None, SLC_BLOCK * KV_ROWS, d), lambda b, t, g=g, s=s: (t[(b * G + g) * n_blk + s], 0, 0))
                for g in range(G) for s in range(n_blk)]
    return pl.pallas_call(
        _dec_sel_kernel,
        name="nsa_dec_sel",
        grid_spec=pltpu.PrefetchScalarGridSpec(
            num_scalar_prefetch=1,
            grid=(DB,),
            in_specs=kv_specs + [pl.BlockSpec((1, G, SUB, d), lambda b, t: (b, 0, 0, 0)),
                                 pl.BlockSpec((KV_ROWS, d), lambda b, t: (b, 0)),
                                 pl.BlockSpec((4, d), lambda b, t: (0, 0))],
            out_specs=pl.BlockSpec((1, G, SUB, d), lambda b, t: (b, 0, 0, 0)),
        ),
        out_shape=jax.ShapeDtypeStruct((DB, G, SUB, d), F32),
        compiler_params=_params("parallel"),
    )(tbl, *([slc_blocks] * (G * n_blk)), q4, kv_new, qk_norm)


def _dec_win_kernel(win_ref, q_ref, new_ref, gate_ref, bg_ref, oc_ref, os_ref, qn_ref, o_ref):
    d = HEAD_DIM
    wlen = win_ref.shape[0] // KV_ROWS
    qgain = qn_ref[0:1, :] * (d ** -0.5)
    kgain = qn_ref[3:4, :]
    lane = lax.broadcasted_iota(jnp.int32, (1, LANES), 1)
    bg = jax.nn.sigmoid(bg_ref[0])
    o_ref[...] = jnp.zeros(o_ref.shape, BF)
    for g in range(KV_HEADS):
        qf = _rms(q_ref[0, g], qgain)
        k = _rms(win_ref[pl.ds(g, wlen, stride=KV_ROWS), :], kgain).astype(BF)
        v = win_ref[pl.ds(KV_HEADS + g, wlen, stride=KV_ROWS), :].astype(BF)
        s = _dot_nt(qf.astype(BF), k)
        s_new = jnp.sum(qf * _rms(new_ref[g:g + 1, :], kgain), axis=-1, keepdims=True)
        m = jnp.maximum(jnp.max(s, axis=-1, keepdims=True), s_new)
        p = jnp.exp(s - m)
        p_new = jnp.exp(s_new - m)
        den = jnp.sum(p, axis=-1, keepdims=True) + p_new
        ow = (_dot(p.astype(BF), v) + p_new * new_ref[KV_HEADS + g:KV_HEADS + g + 1, :]) / den
        oc, os_ = oc_ref[0, g], os_ref[0, g]
        for r in range(GQA):
            base = g * (GQA * 3) + r * 3
            o = (_gate_col(bg, lane, base) * oc[r:r + 1] + _gate_col(bg, lane, base + 1) * os_[r:r + 1]
                 + _gate_col(bg, lane, base + 2) * ow[r:r + 1])
            col = (g * GQA + r) * d
            o_ref[0, 0:1, col:col + d] = (o * _silu(gate_ref[0, :, col:col + d])).astype(BF)


def _dec_win(win, q4, kv_new, gate, zbg, oc, os_, qk_norm):
    DB, G, _, d = q4.shape
    wrows = win.shape[1]
    blk4 = pl.BlockSpec((1, G, SUB, d), lambda b: (b, 0, 0, 0))
    return pl.pallas_call(
        _dec_win_kernel,
        name="nsa_dec_win",
        grid=(DB,),
        in_specs=[pl.BlockSpec((None, wrows, d), lambda b: (b, 0, 0)),
                  blk4,
                  pl.BlockSpec((KV_ROWS, d), lambda b: (b, 0)),
                  pl.BlockSpec((1, 1, D_INNER), lambda b: (b, 0, 0)),
                  pl.BlockSpec((1, 1, LANES), lambda b: (b, 0, 0)),
                  blk4, blk4,
                  pl.BlockSpec((4, d), lambda b: (0, 0))],
        out_specs=pl.BlockSpec((1, SUB, D_INNER), lambda b: (b, 0, 0)),
        out_shape=jax.ShapeDtypeStruct((DB, SUB, D_INNER), BF),
        compiler_params=_params("parallel"),
    )(win, q4, kv_new, gate[:, None, :], zbg[:, None, :], oc, os_, qk_norm)


def _nsa_layer(xp, xs, norm, w_in, qk_norm, cmp_pe, cmp_w1, cmp_w2, w_out, cache_cmp, cache_slc, win, page_table):
    B, T, D = xp.shape
    DB = xs.shape[0]
    G, d = KV_HEADS, HEAD_DIM
    kv0 = D_INNER
    gate0 = D_INNER + 3 * NSA_KV
    n_main = gate0 + D_INNER
    w_qg = jnp.concatenate([w_in[:, :kv0], w_in[:, gate0:n_main]], axis=1).astype(BF)
    w_kv = [w_in[:, kv0 + j * NSA_KV:kv0 + (j + 1) * NSA_KV].astype(BF) for j in range(3)]
    w_tail = _pad_cols(w_in[:, n_main:], LANES).astype(BF)
    w_out_bf = w_out.astype(BF)
    xp2, xs2 = xp.reshape(B * T, D), xs.reshape(DB, D)
    zp, zs = _norm_proj(xp2, norm, w_qg, 512, BF), _norm_proj(xs2, norm, w_qg, 512)
    bgp, bgs = _norm_proj(xp2, norm, w_tail, LANES), _norm_proj(xs2, norm, w_tail, LANES)
    kvp = [_norm_proj_rows(xp2, norm, w) for w in w_kv]
    kvs = [_norm_proj_rows(xs2, norm, w) for w in w_kv]

    w1cat = jnp.concatenate([cmp_w1[:, :CMP_STRIDE], cmp_w1[:, CMP_STRIDE:]], axis=-1).astype(BF)
    w1cat = w1cat.reshape(2, CMP_STRIDE * d, 2 * d)
    w1_flat = cmp_w1.reshape(2, CMP_LEN * d, d).astype(BF)
    pe_flat = jnp.broadcast_to(cmp_pe.transpose(1, 0, 2).reshape(2, 1, CMP_LEN * d), (2, SUB, CMP_LEN * d)).astype(BF)
    w2_bf = cmp_w2.astype(BF)

    n_pages = (B * T) // PAGE_SIZE
    win_p = kvp[2].reshape(B, T, 2, G, d)[:, max(0, T - WINDOW):]
    ab = _cmp_stage1(kvp[0].reshape(n_pages, PAGE_SIZE * KV_ROWS, d), jnp.arange(n_pages, dtype=jnp.int32), w1cat)
    ckp, cvp = _cmp_stage2(ab, B, pe_flat, w1_flat, w2_bf, qk_norm)
    skp, svtp, wkp, wvtp = _kv_prep(kvp[1], kvp[2], qk_norm, B, T)
    op = _nsa_attn_prompt(zp, bgp, ckp, cvp, skp, svtp, wkp, wvtp, qk_norm, B, T)

    past = page_table.shape[1] * PAGE_SIZE
    n_slc = past // SLC_BLOCK + 1
    n_pool = cache_cmp.shape[0]
    ab = _cmp_stage1(cache_cmp.reshape(n_pool, PAGE_SIZE * KV_ROWS, d), page_table.reshape(-1), w1cat)
    cks, cvs = _cmp_stage2(ab, DB, pe_flat, w1_flat, w2_bf, qk_norm)
    q4 = jnp.pad(zs[:, :D_INNER].reshape(DB, G, GQA, d), ((0, 0), (0, 0), (0, SUB - GQA), (0, 0)))
    oc, idx = _dec_cmp(q4, cks, cvs, qk_norm, n_slc)
    idx = idx[:, :, 0, :N_SELECT - 1]
    per_page = PAGE_SIZE // SLC_BLOCK
    phys = jnp.take_along_axis(page_table, (idx // per_page).reshape(DB, -1), axis=1).reshape(idx.shape)
    tbl = (phys * per_page + idx % per_page).reshape(-1).astype(jnp.int32)
    os_ = _dec_sel(tbl, cache_slc.reshape(n_pool * per_page, SLC_BLOCK * KV_ROWS, d), q4, kvs[1], qk_norm, N_SELECT - 1)
    osm = _dec_win(win.reshape(DB, win.shape[1] * KV_ROWS, d), q4, kvs[2], zs[:, D_INNER:], bgs, oc, os_, qk_norm)[:, 0, :]
    win_s = jnp.concatenate([win[:, 1:], kvs[2].reshape(DB, 1, 2, G, d)], axis=1)

    xp = _out_proj(op, w_out_bf, xp2).reshape(B, T, D)
    xs = _out_proj(osm, w_out_bf, xs2).reshape(DB, 1, D)
    as_rows = lambda a, n: a.reshape(-1, n, 2, G, d)
    return (xp, xs, as_rows(kvp[0], T), as_rows(kvs[0], 1), as_rows(kvp[1], T), as_rows(kvs[1], 1), win_p, win_s)


def kernel(x_prompt, x_sample, state_l0_ret, cache_l1_cmp, cache_l1_slc, state_l1_win, state_l2_conv, state_l2_gdn, state_l3_ret, page_table, l0_norm, l0_w_in, l0_gn, l0_w_out, l1_norm, l1_w_in, l1_qk_norm, l1_cmp_pe, l1_cmp_w1, l1_cmp_w2, l1_w_out, l2_norm, l2_w_in, l2_conv_w, l2_A_log, l2_dt_bias, l2_onorm, l2_w_out, l3_norm, l3_w_in, l3_gn, l3_w_out):
    xp, xs, r0p, r0s = _retention_layer(x_prompt, x_sample, l0_norm, l0_w_in, l0_gn, l0_w_out, state_l0_ret)
    xp, xs, cp, cs, slp, sls, wp, ws = _nsa_layer(xp, xs, l1_norm, l1_w_in, l1_qk_norm, l1_cmp_pe, l1_cmp_w1, l1_cmp_w2,
                                                  l1_w_out, cache_l1_cmp, cache_l1_slc, state_l1_win, page_table)
    xp, xs, cvp, cvs, gp, gs = _gdn_layer(xp, xs, l2_norm, l2_w_in, l2_conv_w, l2_A_log, l2_dt_bias, l2_onorm, l2_w_out,
                                          state_l2_conv, state_l2_gdn)
    xp, xs, r3p, r3s = _retention_layer(xp, xs, l3_norm, l3_w_in, l3_gn, l3_w_out, state_l3_ret)
    return (xp, xs, r0p, r0s, cp, cs, slp, sls, wp, ws, cvp, cvs, gp, gs, r3p, r3s)
```
